```python
import math
import jax, jax.numpy as jnp
from jax import lax
import numpy as np

D_MODEL = 1024
BATCH = 2
SEQ = 8192
DEPTH = 2

GRID_W = 64
CTX_LEN = 256
EPS = 1e-6
ROPE_BASE = 10000.0
HEAD_DIM = 64
MIX_WIDTH = D_MODEL
NA_WIDTH = D_MODEL // 2
NA_HEADS = NA_WIDTH // HEAD_DIM
NA_KH_MAX = 8
NA_KW = 16
DIFF_WIDTH = D_MODEL // 4
DIFF_HEADS = 4
DIFF_V_DIM = DIFF_WIDTH // DIFF_HEADS
DIFF_QK_DIM = DIFF_V_DIM // 2
FNET_WIDTH = D_MODEL // 4
FNET_GROUPS = 4
FNET_GROUP_DIM = FNET_WIDTH // FNET_GROUPS
NA_Q0 = 0
NA_K0 = NA_Q0 + NA_WIDTH
NA_V0 = NA_K0 + NA_WIDTH
DQ0 = NA_V0 + NA_WIDTH
DK0 = DQ0 + DIFF_HEADS * 2 * DIFF_QK_DIM
DV0 = DK0 + DIFF_HEADS * 2 * DIFF_QK_DIM
FN0 = DV0 + DIFF_WIDTH
IN_WIDTH = FN0 + FNET_WIDTH
D_FF = 2816
CONV_W = 3
Q_BLOCK = 128

kernel_name = 'hybrid_na_diff_fnet_prefix_block'


def _rmsnorm(x, g):
    x32 = x.astype(jnp.float32)
    y = x32 * lax.rsqrt(jnp.mean(x32 * x32, axis=-1, keepdims=True) + EPS)
    return (y * g.astype(jnp.float32)).astype(x.dtype)


def _modulate(h, shift, scale):
    return h * (1 + scale) + shift


def _rope_1d(x, pos):
    m = x.shape[-1]
    inv = ROPE_BASE ** (-jnp.arange(0, m, 2, dtype=jnp.float32) / m)
    ang = pos.astype(jnp.float32)[:, None] * inv[None, :]
    cos = jnp.cos(ang)[None, :, None, :]
    sin = jnp.sin(ang)[None, :, None, :]
    x32 = x.astype(jnp.float32)
    x1, x2 = x32[..., : m // 2], x32[..., m // 2:]
    return jnp.concatenate([x1 * cos - x2 * sin, x1 * sin + x2 * cos], axis=-1).astype(x.dtype)


def _axial_rope(x, rows, cols):
    d = x.shape[-1]
    return jnp.concatenate([_rope_1d(x[..., : d // 2], rows), _rope_1d(x[..., d // 2:], cols)], axis=-1)


def _split_heads(p, na_q_g, na_k_g, d_q_g, d_k_g):
    B, L, _ = p.shape
    na_q = _rmsnorm(p[..., NA_Q0:NA_K0].reshape(B, L, NA_HEADS, HEAD_DIM), na_q_g)
    na_k = _rmsnorm(p[..., NA_K0:NA_V0].reshape(B, L, NA_HEADS, HEAD_DIM), na_k_g)
    na_v = p[..., NA_V0:DQ0].reshape(B, L, NA_HEADS, HEAD_DIM)
    dq = _rmsnorm(p[..., DQ0:DK0].reshape(B, L, DIFF_HEADS, 2, DIFF_QK_DIM), d_q_g)
    dk = _rmsnorm(p[..., DK0:DV0].reshape(B, L, DIFF_HEADS, 2, DIFF_QK_DIM), d_k_g)
    dv = p[..., DV0:FN0].reshape(B, L, DIFF_HEADS, DIFF_V_DIM)
    fu = p[..., FN0:IN_WIDTH]
    return na_q, na_k, na_v, dq, dk, dv, fu


def _neighbourhood_attention(q, k, v, kc, vc, rpb, n_rows):
    B, S, H, Dh = q.shape
    kh = min(NA_KH_MAX, n_rows)
    n_keys = kh * NA_KW
    scale = Dh ** -0.5
    cols = jnp.arange(GRID_W)
    col_start = jnp.clip(cols - NA_KW // 2, 0, GRID_W - NA_KW)
    key_cols = col_start[:, None] + jnp.arange(NA_KW)[None, :]
    dc = key_cols - cols[:, None] + (NA_KW - 1)
    q_rows = q.reshape(B, n_rows, GRID_W, H, Dh).transpose(1, 0, 2, 3, 4)

    def row_block(args):
        r, q_r = args
        row_start = jnp.clip(r - kh // 2, 0, n_rows - kh)
        key_rows = row_start + jnp.arange(kh)
        idx = (key_rows[None, :, None] * GRID_W + key_cols[:, None, :]).reshape(GRID_W, n_keys)
        k_g = k[:, idx]
        v_g = v[:, idx]
        dr = key_rows - r + (NA_KH_MAX - 1)
        bias = rpb[:, dr[None, :, None], dc[:, None, :]].reshape(H, GRID_W, n_keys)
        s_loc = jnp.einsum('bqhd,bqnhd->bhqn', q_r, k_g).astype(jnp.float32) * scale + bias[None].astype(jnp.float32)
        s_ctx = jnp.einsum('bqhd,bchd->bhqc', q_r, kc).astype(jnp.float32) * scale
        p = jax.nn.softmax(jnp.concatenate([s_loc, s_ctx], axis=-1), axis=-1).astype(v.dtype)
        return (jnp.einsum('bhqn,bqnhd->bqhd', p[..., :n_keys], v_g)
                + jnp.einsum('bhqc,bchd->bqhd', p[..., n_keys:], vc))

    out = lax.map(row_block, (jnp.arange(n_rows), q_rows))
    return out.transpose(1, 0, 2, 3, 4).reshape(B, S, H, Dh)


def _dense_attention(q, k, v):
    s = jnp.einsum('bqhd,bkhd->bhqk', q, k).astype(jnp.float32) * q.shape[-1] ** -0.5
    p = jax.nn.softmax(s, axis=-1).astype(v.dtype)
    return jnp.einsum('bhqk,bkhd->bqhd', p, v)


def _diff_attend(q1, q2, k1, k2, v, lam):
    scale = q1.shape[-1] ** -0.5
    s1 = jnp.einsum('bqhd,bkhd->bhqk', q1, k1).astype(jnp.float32) * scale
    s2 = jnp.einsum('bqhd,bkhd->bhqk', q2, k2).astype(jnp.float32) * scale
    p = jax.nn.softmax(s1, axis=-1) - lam * jax.nn.softmax(s2, axis=-1)
    return jnp.einsum('bhqk,bkhe->bqhe', p.astype(v.dtype), v)


def _diff_attention_latent(q1, q2, k1, k2, v, k1c, k2c, vc, lam):
    B, S, H, d = q1.shape
    kk1 = jnp.concatenate([k1, k1c], axis=1)
    kk2 = jnp.concatenate([k2, k2c], axis=1)
    vv = jnp.concatenate([v, vc], axis=1)
    nb = S // Q_BLOCK
    q1b = q1.reshape(B, nb, Q_BLOCK, H, d).transpose(1, 0, 2, 3, 4)
    q2b = q2.reshape(B, nb, Q_BLOCK, H, d).transpose(1, 0, 2, 3, 4)
    out = lax.map(lambda a: _diff_attend(a[0], a[1], kk1, kk2, vv, lam), (q1b, q2b))
    return out.transpose(1, 0, 2, 3, 4).reshape(B, S, H, v.shape[-1])


def _fourier(u):
    B, L, _ = u.shape
    g = u.reshape(B, L, FNET_GROUPS, FNET_GROUP_DIM).astype(jnp.float32)
    f = jnp.fft.fftn(g, axes=(1, 3), norm='ortho').real
    return f.reshape(B, L, FNET_WIDTH).astype(u.dtype)


def _merge(na_o, diff_o, four_u, subln_g, lam_init, w_four, w_out):
    B, L = na_o.shape[:2]
    diff_o = _rmsnorm(diff_o, subln_g) * (1.0 - lam_init)
    four = _fourier(four_u) @ w_four
    y = jnp.concatenate([na_o.reshape(B, L, NA_WIDTH), diff_o.reshape(B, L, DIFF_WIDTH), four], axis=-1)
    return y @ w_out


def _conv_ffn(h, w_up, conv_w, conv_b, w_down):
    u = h @ w_up
    up = jnp.pad(u, ((0, 0), (1, 1), (0, 0)))
    u = up[:, :-2] * conv_w[0] + up[:, 1:-1] * conv_w[1] + up[:, 2:] * conv_w[2] + conv_b
    gate, val = jnp.split(u, 2, axis=-1)
    return (jax.nn.silu(gate) * val) @ w_down


def setup_inputs(seed: int = 0) -> dict:
    key = jax.random.key(seed)
    ks = jax.random.split(key, 22)
    D = D_MODEL

    def nrm(k, shape, s):
        return jax.random.normal(k, shape, jnp.float32) * s

    centre = (jnp.arange(CONV_W) == CONV_W // 2).astype(jnp.float32)[None, :, None]
    return {
        'x': nrm(ks[0], (BATCH, SEQ, D), 1.0),
        'c': nrm(ks[1], (BATCH, D), 1.0),
        'ctx': nrm(ks[2], (BATCH, CTX_LEN, D), 1.0),
        'c_ctx': nrm(ks[3], (D,), 1.0),
        'w_mod': nrm(ks[4], (DEPTH, D, 6 * D), 0.5 * D ** -0.5),
        'b_mod': nrm(ks[5], (DEPTH, 6 * D), 0.02),
        'norm1_g': 1.0 + nrm(ks[6], (DEPTH, D), 0.05),
        'w_in': nrm(ks[7], (DEPTH, D, IN_WIDTH), D ** -0.5),
        'na_q_g': 1.0 + nrm(ks[8], (DEPTH, HEAD_DIM), 0.05),
        'na_k_g': 1.0 + nrm(ks[9], (DEPTH, HEAD_DIM), 0.05),
        'na_rpb': nrm(ks[10], (DEPTH, NA_HEADS, 2 * NA_KH_MAX - 1, 2 * NA_KW - 1), 0.1),
        'diff_q_g': 1.0 + nrm(ks[11], (DEPTH, DIFF_QK_DIM), 0.05),
        'diff_k_g': 1.0 + nrm(ks[12], (DEPTH, DIFF_QK_DIM), 0.05),
        'diff_lambda': nrm(ks[13], (DEPTH, 4, DIFF_QK_DIM), 0.1),
        'diff_subln_g': 1.0 + nrm(ks[14], (DEPTH, DIFF_V_DIM), 0.05),
        'w_fourier': nrm(ks[15], (DEPTH, FNET_WIDTH, FNET_WIDTH), FNET_WIDTH ** -0.5),
        'w_out': nrm(ks[16], (DEPTH, MIX_WIDTH, D), MIX_WIDTH ** -0.5),
        'norm2_g': 1.0 + nrm(ks[17], (DEPTH, D), 0.05),
        'w_up': nrm(ks[18], (DEPTH, D, 2 * D_FF), D ** -0.5),
        'conv_w': centre + nrm(ks[19], (DEPTH, CONV_W, 2 * D_FF), 0.3),
        'conv_b': nrm(ks[20], (DEPTH, 2 * D_FF), 0.02),
        'w_down': nrm(ks[21], (DEPTH, D_FF, D), D_FF ** -0.5),
    }


def reference(x, c, ctx, c_ctx, w_mod, b_mod, norm1_g, w_in, na_q_g, na_k_g, na_rpb,
              diff_q_g, diff_k_g, diff_lambda, diff_subln_g, w_fourier, w_out, norm2_g,
              w_up, conv_w, conv_b, w_down):
    B, S, D = x.shape
    n_rows = S // GRID_W
    pos = jnp.arange(S)
    rows, cols = pos // GRID_W, pos % GRID_W
    cx = ctx
    for l in range(DEPTH):
        lam_init = 0.8 - 0.6 * math.exp(-0.3 * l)
        lq1, lk1, lq2, lk2 = (diff_lambda[l, i].astype(jnp.float32) for i in range(4))
        lam = jnp.exp(jnp.sum(lq1 * lk1)) - jnp.exp(jnp.sum(lq2 * lk2)) + lam_init

        mod_x = jax.nn.silu(c) @ w_mod[l] + b_mod[l]
        mod_c = jax.nn.silu(c_ctx) @ w_mod[l] + b_mod[l]
        sh1, sc1, g1, sh2, sc2, g2 = jnp.split(mod_x[:, None, :], 6, axis=-1)
        csh1, csc1, cg1, csh2, csc2, cg2 = jnp.split(mod_c, 6)

        hx = _modulate(_rmsnorm(x, norm1_g[l]), sh1, sc1)
        hc = _modulate(_rmsnorm(cx, norm1_g[l]), csh1, csc1)
        nqx, nkx, nvx, dqx, dkx, dvx, fux = _split_heads(hx @ w_in[l], na_q_g[l], na_k_g[l], diff_q_g[l], diff_k_g[l])
        nqc, nkc, nvc, dqc, dkc, dvc, fuc = _split_heads(hc @ w_in[l], na_q_g[l], na_k_g[l], diff_q_g[l], diff_k_g[l])

        na_x = _neighbourhood_attention(nqx, nkx, nvx, nkc, nvc, na_rpb[l], n_rows)
        q1x = _axial_rope(dqx[..., 0, :], rows, cols)
        q2x = _axial_rope(dqx[..., 1, :], rows, cols)
        k1x = _axial_rope(dkx[..., 0, :], rows, cols)
        k2x = _axial_rope(dkx[..., 1, :], rows, cols)
        diff_x = _diff_attention_latent(q1x, q2x, k1x, k2x, dvx, dkc[..., 0, :], dkc[..., 1, :], dvc, lam)
        x = x + g1 * _merge(na_x, diff_x, fux, diff_subln_g[l], lam_init, w_fourier[l], w_out[l])
        x = x + g2 * _conv_ffn(_modulate(_rmsnorm(x, norm2_g[l]), sh2, sc2), w_up[l], conv_w[l], conv_b[l], w_down[l])

        if l < DEPTH - 1:
            na_c = _dense_attention(nqc, nkc, nvc)
            diff_c = _diff_attend(dqc[..., 0, :], dqc[..., 1, :], dkc[..., 0, :], dkc[..., 1, :], dvc, lam)
            cx = cx + cg1 * _merge(na_c, diff_c, fuc, diff_subln_g[l], lam_init, w_fourier[l], w_out[l])
            cx = cx + cg2 * _conv_ffn(_modulate(_rmsnorm(cx, norm2_g[l]), csh2, csc2), w_up[l], conv_w[l], conv_b[l], w_down[l])
    return x
```

```python
import functools
import math

import numpy as np
import jax
import jax.numpy as jnp
from jax import lax
from jax.experimental import pallas as pl
from jax.experimental.pallas import tpu as pltpu

F32 = jnp.float32
BF16 = jnp.bfloat16

GRID_W = 64
EPS = 1e-6
ROPE_BASE = 10000.0
HEAD_DIM = 64
NA_HEADS = 8
NA_WIDTH = NA_HEADS * HEAD_DIM
NA_KH = 8
NA_KW = 16
DIFF_HEADS = 4
DIFF_QK_DIM = 32
DIFF_V_DIM = 64
DIFF_WIDTH = DIFF_HEADS * DIFF_V_DIM
FNET_GROUPS = 4
FNET_GROUP_DIM = 64
FNET_WIDTH = FNET_GROUPS * FNET_GROUP_DIM
NA_Q0 = 0
NA_K0 = NA_Q0 + NA_WIDTH
NA_V0 = NA_K0 + NA_WIDTH
DQ0 = NA_V0 + NA_WIDTH
DK0 = DQ0 + DIFF_WIDTH
DV0 = DK0 + DIFF_WIDTH
FN0 = DV0 + DIFF_WIDTH
IN_WIDTH = FN0 + FNET_WIDTH
NEG_BIG = -1e30
LOG2E = 1.4426950408889634

VMEM_LIMIT_BYTES = 52 * 1024 * 1024
LANES = 128
BF16_SUBLANES = 16


def _params(*sem):
    return pltpu.CompilerParams(dimension_semantics=sem, vmem_limit_bytes=VMEM_LIMIT_BYTES)


def _dot(a, b):
    return jnp.dot(a, b, preferred_element_type=F32)


def _dot_nt(a, b):
    return lax.dot_general(a, b, (((1,), (1,)), ((), ())), preferred_element_type=F32)


def _sigmoid(x):
    return 1.0 / (1.0 + jnp.exp(-x))


def _group_mean_matrix(width, group):
    idx = np.arange(width) // group
    return jnp.asarray((idx[:, None] == idx[None, :]).astype(np.float32) / group, dtype=BF16)


def _mod_kernel(c_ref, w_ref, b_ref, o_ref):
    c = c_ref[...]
    s = c * _sigmoid(c)
    s_hi = s.astype(BF16)
    s_lo = (s - s_hi.astype(F32)).astype(BF16)
    w = w_ref[0]
    w_hi = w.astype(BF16)
    w_lo = (w - w_hi.astype(F32)).astype(BF16)
    o_ref[0] = _dot(s_hi, w_hi) + _dot(s_hi, w_lo) + _dot(s_lo, w_hi) + b_ref[0]


def _modulation(cc, w_mod, b_mod):
    depth, d, n = w_mod.shape
    tn = 768
    return pl.pallas_call(
        _mod_kernel,
        grid=(depth, n // tn),
        in_specs=[
            pl.BlockSpec((8, d), lambda l, j: (0, 0)),
            pl.BlockSpec((1, d, tn), lambda l, j: (l, 0, j)),
            pl.BlockSpec((1, 1, tn), lambda l, j: (l, 0, j)),
        ],
        out_specs=pl.BlockSpec((1, 8, tn), lambda l, j: (l, 0, j)),
        out_shape=jax.ShapeDtypeStruct((depth, 8, n), F32),
        compiler_params=_params("parallel", "parallel"),
        name="modulation",
    )(cc, w_mod, b_mod.reshape(depth, 1, n))


def _inproj_kernel(x_ref, sh_ref, sc_ref, g_ref, w_ref, gq_ref, gk_ref, dgq_ref, dgk_ref,
                   g64_ref, g32_ref, cos_ref, sin_ref,
                   naq_ref, nak_ref, nav_ref, dqt_ref, dk_ref, dvt_ref, fu_ref):
    x = x_ref[0]
    ms = jnp.mean(x * x, axis=-1, keepdims=True)
    h = x * lax.rsqrt(ms + EPS) * g_ref[...]
    h = h * (1.0 + sc_ref[0]) + sh_ref[0]
    hb = h.astype(BF16)

    def proj(c0, c1):
        return _dot(hb, w_ref[:, c0:c1])

    def group_norm(p, gmat_ref, gain):
        gms = _dot((p * p).astype(BF16), gmat_ref[...])
        return p * lax.rsqrt(gms + EPS) * gain

    def rope(y):
        lane = lax.broadcasted_iota(jnp.int32, y.shape, 1)
        nxt = pltpu.roll(y, DIFF_WIDTH - 8, axis=1)
        prv = pltpu.roll(y, 8, axis=1)
        partner = jnp.where((lane & 8) == 0, nxt, prv)
        return y * cos_ref[...] + partner * sin_ref[...]

    q = group_norm(proj(NA_Q0, NA_K0), g64_ref, gq_ref[...]) * (HEAD_DIM ** -0.5)
    naq_ref[0] = q.astype(BF16)
    k = group_norm(proj(NA_K0, NA_V0), g64_ref, gk_ref[...])
    nak_ref[0] = k.astype(BF16)
    nav_ref[0] = proj(NA_V0, DQ0).astype(BF16)
    dq = rope(group_norm(proj(DQ0, DK0), g32_ref, dgq_ref[...])) * (DIFF_QK_DIM ** -0.5 * LOG2E)
    dqt_ref[0] = dq.T.astype(BF16)
    dk = rope(group_norm(proj(DK0, DV0), g32_ref, dgk_ref[...]))
    dk_ref[0] = dk.astype(BF16)
    dvt_ref[0, 0] = proj(DV0, FN0).T.astype(BF16)
    fu_ref[0] = proj(FN0, IN_WIDTH)


def _in_projection(x, shift, scale, norm_g, w_in, gq, gk, dgq, dgk, cos_t, sin_t, tm):
    b, s, d = x.shape
    g64 = _group_mean_matrix(NA_WIDTH, HEAD_DIM)
    g32 = _group_mean_matrix(DIFF_WIDTH, DIFF_QK_DIM)
    const = lambda shape: pl.BlockSpec(shape, lambda bi, i: (0,) * len(shape))
    per_b = pl.BlockSpec((1, 1, d), lambda bi, i: (bi, 0, 0))
    tok = lambda w: pl.BlockSpec((1, tm, w), lambda bi, i: (bi, i, 0))
    tok_t = lambda w: pl.BlockSpec((1, w, tm), lambda bi, i: (bi, 0, i))
    sds = jax.ShapeDtypeStruct
    return pl.pallas_call(
        _inproj_kernel,
        grid=(b, s // tm),
        in_specs=[
            tok(d), per_b, per_b, const((1, d)), const((d, IN_WIDTH)),
            const((1, NA_WIDTH)), const((1, NA_WIDTH)), const((1, DIFF_WIDTH)), const((1, DIFF_WIDTH)),
            const((NA_WIDTH, NA_WIDTH)), const((DIFF_WIDTH, DIFF_WIDTH)),
            pl.BlockSpec((tm, DIFF_WIDTH), lambda bi, i: (i, 0)),
            pl.BlockSpec((tm, DIFF_WIDTH), lambda bi, i: (i, 0)),
        ],
        out_specs=[tok(NA_WIDTH), tok(NA_WIDTH), tok(NA_WIDTH),
                   tok_t(DIFF_WIDTH), tok(DIFF_WIDTH),
                   pl.BlockSpec((1, 1, DIFF_WIDTH, tm), lambda bi, i: (bi, i, 0, 0)), tok(FNET_WIDTH)],
        out_shape=[sds((b, s, NA_WIDTH), BF16), sds((b, s, NA_WIDTH), BF16), sds((b, s, NA_WIDTH), BF16),
                   sds((b, DIFF_WIDTH, s), BF16), sds((b, s, DIFF_WIDTH), BF16),
                   sds((b, s // tm, DIFF_WIDTH, tm), BF16), sds((b, s, FNET_WIDTH), F32)],
        compiler_params=_params("parallel", "parallel"),
        name="in_projection",
    )(x, shift, scale, norm_g, w_in, gq, gk, dgq, dgk, g64, g32, cos_t, sin_t)


def _rope_tables(s):
    pos = jnp.arange(s)
    rows = (pos // GRID_W).astype(F32)
    cols = (pos % GRID_W).astype(F32)
    m = DIFF_QK_DIM // 2
    inv = ROPE_BASE ** (-jnp.arange(0, m, 2, dtype=F32) / m)
    ang_r = rows[:, None] * inv[None, :]
    ang_c = cols[:, None] * inv[None, :]
    ang = jnp.concatenate([ang_r, ang_r, ang_c, ang_c], axis=1)
    sign = jnp.asarray(np.tile(np.repeat([-1.0, 1.0], 8), 2), F32)
    cos32 = jnp.cos(ang)
    sin32 = jnp.sin(ang) * sign[None, :]
    reps = DIFF_WIDTH // DIFF_QK_DIM
    return jnp.tile(cos32, (1, reps)), jnp.tile(sin32, (1, reps))


NA_GROUP_ROWS = 8
NA_HALF = NA_WIDTH // 2
NA_HEADS_PER_HALF = NA_HALF // HEAD_DIM


def _head_block_mask(rows_per_head, n_heads, width):
    r = lax.broadcasted_iota(jnp.int32, (rows_per_head * n_heads, width), 0) // rows_per_head
    c = lax.broadcasted_iota(jnp.int32, (rows_per_head * n_heads, width), 1) // (width // n_heads)
    return r == c


def _na_kernel(q_ref, kp_ref, k0_ref, kn_ref, vp_ref, v0_ref, vn_ref, kc_ref, vc_ref, bias_ref,
               o_ref, kbuf, vbuf):
    g = pl.program_id(1)
    blk = NA_GROUP_ROWS * GRID_W
    kbuf[0:blk] = kp_ref[0]
    kbuf[blk:2 * blk] = k0_ref[0]
    kbuf[2 * blk:3 * blk] = kn_ref[0]
    vbuf[0:blk] = vp_ref[0]
    vbuf[blk:2 * blk] = v0_ref[0]
    vbuf[2 * blk:3 * blk] = vn_ref[0]
    n_rows = pl.num_programs(1) * NA_GROUP_ROWS
    head_mask = _head_block_mask(GRID_W, NA_HEADS_PER_HALF, NA_HALF)
    n_keys = NA_KH * GRID_W

    def row_body(i, carry):
        r = g * NA_GROUP_ROWS + i
        row_start = jnp.clip(r - NA_KH // 2, 0, n_rows - NA_KH)
        off = pl.multiple_of((row_start - g * NA_GROUP_ROWS + NA_GROUP_ROWS) * GRID_W, GRID_W)
        d0 = row_start - r + (NA_KH - 1)
        qoff = pl.multiple_of(i * GRID_W, GRID_W)
        for hh in range(2):
            cs = slice(hh * NA_HALF, (hh + 1) * NA_HALF)
            qh = q_ref[0, pl.ds(qoff, GRID_W), cs]
            qst = jnp.concatenate([qh] * NA_HEADS_PER_HALF, axis=0)
            qst = jnp.where(head_mask, qst, jnp.zeros_like(qst))
            kw = kbuf[pl.ds(off, n_keys), cs]
            vw = vbuf[pl.ds(off, n_keys), cs]
            bias = jnp.concatenate([bias_ref[d0 + 2 * j, hh] for j in range(NA_KH // 2)], axis=1)
            s_loc = _dot_nt(qst, kw) + bias
            s_ctx = _dot_nt(qst, kc_ref[0, :, cs])
            m = jnp.maximum(jnp.max(s_loc, axis=-1, keepdims=True), jnp.max(s_ctx, axis=-1, keepdims=True))
            e_loc = jnp.exp(s_loc - m)
            e_ctx = jnp.exp(s_ctx - m)
            l = jnp.sum(e_loc, axis=-1, keepdims=True) + jnp.sum(e_ctx, axis=-1, keepdims=True)
            o = _dot(e_loc.astype(BF16), vw) + _dot(e_ctx.astype(BF16), vc_ref[0, :, cs])
            o = jnp.where(head_mask, o * (1.0 / l), 0.0)
            out = o[0:GRID_W]
            for hl in range(1, NA_HEADS_PER_HALF):
                out = out + o[hl * GRID_W:(hl + 1) * GRID_W]
            o_ref[0, pl.ds(qoff, GRID_W), cs] = out.astype(o_ref.dtype)
        return carry

    lax.fori_loop(0, NA_GROUP_ROWS, row_body, 0)


def _na_bias_table(rpb):
    h, n_dr, _ = rpb.shape
    qc = np.arange(GRID_W)
    kc = np.arange(GRID_W)
    col_start = np.clip(qc - NA_KW // 2, 0, GRID_W - NA_KW)
    valid = (kc[None, :] >= col_start[:, None]) & (kc[None, :] < col_start[:, None] + NA_KW)
    pad = GRID_W - NA_KW
    period = 2 * GRID_W - 1
    rext = jnp.pad(rpb, ((0, 0), (0, 0), (pad, pad)))
    flat = jnp.tile(rext, (1, 1, GRID_W + 1))[:, :, :GRID_W * (period + 1)]
    hankel = flat.reshape(h, n_dr, GRID_W, period + 1)[..., :GRID_W]
    toep = hankel[:, :, ::-1, :]
    toep = jnp.where(jnp.asarray(valid)[None, None], toep, NEG_BIG)
    pair = jnp.concatenate([toep[:, :-1], toep[:, 1:]], axis=-1)
    pair = pair.reshape(2, NA_HEADS_PER_HALF, n_dr - 1, GRID_W, 2 * GRID_W).transpose(2, 0, 1, 3, 4)
    return pair.reshape(n_dr - 1, 2, NA_HEADS_PER_HALF * GRID_W, 2 * GRID_W).astype(F32)


def _neighbourhood_attention(q, k, v, kc, vc, rpb):
    b, s, _ = q.shape
    n_groups = s // (NA_GROUP_ROWS * GRID_W)
    blk = NA_GROUP_ROWS * GRID_W
    c = kc.shape[1]
    bias = _na_bias_table(rpb)
    cur = pl.BlockSpec((1, blk, NA_WIDTH), lambda bi, g: (bi, g, 0))
    prv = pl.BlockSpec((1, blk, NA_WIDTH), lambda bi, g: (bi, jnp.maximum(g - 1, 0), 0))
    nxt = pl.BlockSpec((1, blk, NA_WIDTH), lambda bi, g: (bi, jnp.minimum(g + 1, n_groups - 1), 0))
    ctx = pl.BlockSpec((1, c, NA_WIDTH), lambda bi, g: (bi, 0, 0))
    return pl.pallas_call(
        _na_kernel,
        grid=(b, n_groups),
        in_specs=[cur, prv, cur, nxt, prv, cur, nxt, ctx, ctx,
                  pl.BlockSpec(bias.shape, lambda bi, g: (0, 0, 0, 0))],
        out_specs=cur,
        out_shape=jax.ShapeDtypeStruct((b, s, NA_WIDTH), BF16),
        scratch_shapes=[pltpu.VMEM((3 * blk, NA_WIDTH), BF16), pltpu.VMEM((3 * blk, NA_WIDTH), BF16)],
        compiler_params=_params("parallel", "arbitrary"),
        name="neighbourhood_attention",
    )(q, k, k, k, v, v, v, kc, vc, bias)


def _dense_na_kernel(q_ref, k_ref, v_ref, o_ref):
    c = q_ref.shape[1]
    head_mask = _head_block_mask(c, NA_HEADS_PER_HALF, NA_HALF)
    for hh in range(2):
        cs = slice(hh * NA_HALF, (hh + 1) * NA_HALF)
        qh = q_ref[0, :, cs]
        qst = jnp.concatenate([qh] * NA_HEADS_PER_HALF, axis=0)
        qst = jnp.where(head_mask, qst, jnp.zeros_like(qst))
        s = _dot_nt(qst, k_ref[0, :, cs])
        m = jnp.max(s, axis=-1, keepdims=True)
        e = jnp.exp(s - m)
        l = jnp.sum(e, axis=-1, keepdims=True)
        o = _dot(e.astype(BF16), v_ref[0, :, cs])
        o = jnp.where(head_mask, o * (1.0 / l), 0.0)
        out = o[0:c]
        for hl in range(1, NA_HEADS_PER_HALF):
            out = out + o[hl * c:(hl + 1) * c]
        o_ref[0, :, cs] = out.astype(o_ref.dtype)


def _dense_na_attention(q, k, v):
    b, c, w = q.shape
    spec = pl.BlockSpec((1, c, w), lambda bi: (bi, 0, 0))
    return pl.pallas_call(
        _dense_na_kernel,
        grid=(b,),
        in_specs=[spec, spec, spec],
        out_specs=spec,
        out_shape=jax.ShapeDtypeStruct((b, c, w), BF16),
        compiler_params=_params("parallel"),
        name="context_dense_attention",
    )(q, k, v)


N_DIFF_STREAMS = 2 * DIFF_HEADS


DIFF_HEAD_PAIRS = DIFF_HEADS // 2


def _diff_kernel(*refs, lam_init, has_ctx):
    if has_ctx:
        lam_ref, qt_ref, k_ref, vt_ref, kc_ref, vct_ref, o_ref, qs, s_a, s_b, m_scr, l_scr, acc = refs
    else:
        lam_ref, qt_ref, k_ref, vt_ref, o_ref, qs, s_a, s_b, m_scr, l_scr, acc = refs
    s_bufs = (s_a, s_b)
    tq = qt_ref.shape[2]
    n_chunks, _, tk = vt_ref.shape[1:]
    pair_w = 4 * tq

    qt = qt_ref[0]
    stream = lax.broadcasted_iota(jnp.int32, qt.shape, 0) // DIFF_QK_DIM
    for st in range(N_DIFF_STREAMS):
        qs[:, st * tq:(st + 1) * tq] = jnp.where(stream == st, qt, jnp.zeros_like(qt))
    m_scr[...] = jnp.full(m_scr.shape, NEG_BIG, F32)
    l_scr[...] = jnp.zeros(l_scr.shape, F32)
    acc[...] = jnp.zeros(acc.shape, F32)

    def scores(kk, s_buf):
        n = kk.shape[0]
        for pair in range(DIFF_HEAD_PAIRS):
            cols = slice(pair * pair_w, (pair + 1) * pair_w)
            s_buf[0:n, cols] = _dot(kk, qs[:, cols])

    def absorb(s_buf, n, vt_rows):
        for pair in range(DIFF_HEAD_PAIRS):
            cols = slice(pair * pair_w, (pair + 1) * pair_w)
            s = s_buf[0:n, cols]
            m_old = m_scr[:, cols]
            m_new = jnp.maximum(m_old, jnp.max(s, axis=0, keepdims=True))
            alpha = jnp.exp2(m_old - m_new)
            p = jnp.exp2(s - m_new)
            l_scr[:, cols] = alpha * l_scr[:, cols] + jnp.sum(p, axis=0, keepdims=True)
            m_scr[:, cols] = m_new
            pb = p.astype(BF16)
            for hh in range(2):
                h = 2 * pair + hh
                rows = slice(h * DIFF_V_DIM, (h + 1) * DIFF_V_DIM)
                hc = slice(hh * 2 * tq, (hh + 1) * 2 * tq)
                acc[rows, :] = acc[rows, :] * alpha[:, hc] + _dot(vt_rows(rows), pb[:, hc])

    n_items = n_chunks + (1 if has_ctx else 0)

    def scores_item(t, buf):
        if isinstance(t, int) and t == n_chunks:
            scores(kc_ref[0], buf)
        else:
            start = t * tk if isinstance(t, int) else pl.multiple_of(t * tk, tk)
            scores(k_ref[0, pl.ds(start, tk), :], buf)

    def absorb_item(t, buf):
        if isinstance(t, int) and t == n_chunks:
            absorb(buf, kc_ref.shape[1], lambda rows: vct_ref[0, rows, :])
        else:
            absorb(buf, tk, lambda rows: vt_ref[0, t, rows, :])

    scores_item(0, s_a)
    n_pairs = (n_chunks - 1) // 2

    def body(i, carry):
        t = 2 * i
        scores_item(t + 1, s_b)
        absorb_item(t, s_a)
        scores_item(t + 2, s_a)
        absorb_item(t + 1, s_b)
        return carry

    lax.fori_loop(0, n_pairs, body, 0)
    for t in range(2 * n_pairs, n_items):
        if t + 1 < n_items:
            scores_item(t + 1, s_bufs[(t + 1) % 2])
        absorb_item(t, s_bufs[t % 2])

    lp = lam_ref[...]
    lam = (jnp.exp(jnp.sum(lp[0:1] * lp[1:2], axis=1, keepdims=True))
           - jnp.exp(jnp.sum(lp[2:3] * lp[3:4], axis=1, keepdims=True)) + lam_init)
    inv_l = 1.0 / l_scr[...]
    outs = []
    for h in range(DIFF_HEADS):
        rows = slice(h * DIFF_V_DIM, (h + 1) * DIFF_V_DIM)
        o1 = acc[rows, 0:tq] * inv_l[:, 2 * h * tq:(2 * h + 1) * tq]
        o2 = acc[rows, tq:2 * tq] * inv_l[:, (2 * h + 1) * tq:(2 * h + 2) * tq]
        outs.append(o1 - lam * o2)
    o_ref[0] = jnp.concatenate(outs, axis=0).T


def _diff_attention(lam_p, qt, k, vt, lam_init, tq, ctx_kv=None):
    b, w, sq = qt.shape
    sk = k.shape[1]
    n_chunks, _, tk = vt.shape[1:]
    whole = lambda a: pl.BlockSpec((1,) + a.shape[1:], lambda bi, i: (bi,) + (0,) * (a.ndim - 1))
    operands = [lam_p, qt, k, vt]
    in_specs = [
        pl.BlockSpec(lam_p.shape, lambda bi, i: (0, 0)),
        pl.BlockSpec((1, w, tq), lambda bi, i: (bi, 0, i)),
        whole(k), whole(vt),
    ]
    if ctx_kv is not None:
        operands += list(ctx_kv)
        in_specs += [whole(ctx_kv[0]), whole(ctx_kv[1])]
    return pl.pallas_call(
        functools.partial(_diff_kernel, lam_init=lam_init, has_ctx=ctx_kv is not None),
        grid=(b, sq // tq),
        in_specs=in_specs,
        out_specs=pl.BlockSpec((1, tq, w), lambda bi, i: (bi, i, 0)),
        out_shape=jax.ShapeDtypeStruct((b, sq, w), F32),
        scratch_shapes=[
            pltpu.VMEM((w, N_DIFF_STREAMS * tq), BF16),
            pltpu.VMEM((tk, N_DIFF_STREAMS * tq), F32),
            pltpu.VMEM((tk, N_DIFF_STREAMS * tq), F32),
            pltpu.VMEM((1, N_DIFF_STREAMS * tq), F32),
            pltpu.VMEM((1, N_DIFF_STREAMS * tq), F32),
            pltpu.VMEM((w, 2 * tq), F32),
        ],
        compiler_params=_params("parallel", "arbitrary"),
        name="differential_attention",
    )(*operands)


FFT_N1 = 64
FFT_N1_STEP = 8


def _dft_tables(n):
    idx = np.arange(n)
    ang = 2.0 * np.pi * ((idx[:, None] * idx[None, :]) % n) / n
    return np.cos(ang), np.sin(ang)


def _fft_stage_a_kernel(x_ref, t_ref, twc_ref, tws_ref, ur_ref, ui_ref):
    n2 = x_ref.shape[1]
    t = t_ref[...].astype(BF16)
    for i in range(FFT_N1_STEP):
        xs = x_ref[0, :, i * FNET_WIDTH:(i + 1) * FNET_WIDTH].astype(BF16)
        u = _dot(t, xs)
        ur, ui = u[0:n2], u[n2:2 * n2]
        c = jnp.concatenate([twc_ref[i]] * (FNET_WIDTH // LANES), axis=1)
        s = jnp.concatenate([tws_ref[i]] * (FNET_WIDTH // LANES), axis=1)
        ur_ref[0, i] = ur * c + ui * s
        ui_ref[0, i] = ui * c - ur * s


def _fft_stage_c_kernel(ur_ref, ui_ref, c_ref, s_ref, pr_ref, pi_ref):
    ur = ur_ref[0].astype(BF16)
    ui = ui_ref[0].astype(BF16)
    c1 = c_ref[...].astype(BF16)
    s1 = s_ref[...].astype(BF16)
    pr_ref[0] = _dot(c1, ur) + _dot(s1, ui)
    pi_ref[0] = _dot(c1, ui) - _dot(s1, ur)


def _position_dft(fu):
    b, l, w = fu.shape
    n1, n2 = FFT_N1, l // FFT_N1
    scale = 1.0 / math.sqrt(l * FNET_GROUP_DIM)
    c2, s2 = _dft_tables(n2)
    t2 = jnp.asarray(np.concatenate([c2, -s2], axis=0) * scale, dtype=F32)
    ang = (2.0 * jnp.pi / l) * (jnp.arange(n1, dtype=F32)[:, None] * jnp.arange(n2, dtype=F32)[None, :])
    twc = jnp.broadcast_to(jnp.cos(ang)[:, :, None], (n1, n2, LANES))
    tws = jnp.broadcast_to(jnp.sin(ang)[:, :, None], (n1, n2, LANES))
    xv = fu.reshape(b, n2, n1 * w)
    sds = jax.ShapeDtypeStruct
    ur, ui = pl.pallas_call(
        _fft_stage_a_kernel,
        grid=(b, n1 // FFT_N1_STEP),
        in_specs=[
            pl.BlockSpec((1, n2, FFT_N1_STEP * w), lambda bi, j: (bi, 0, j)),
            pl.BlockSpec((2 * n2, n2), lambda bi, j: (0, 0)),
            pl.BlockSpec((FFT_N1_STEP, n2, LANES), lambda bi, j: (j, 0, 0)),
            pl.BlockSpec((FFT_N1_STEP, n2, LANES), lambda bi, j: (j, 0, 0)),
        ],
        out_specs=[pl.BlockSpec((1, FFT_N1_STEP, n2, w), lambda bi, j: (bi, j, 0, 0))] * 2,
        out_shape=[sds((b, n1, n2, w), F32)] * 2,
        compiler_params=_params("parallel", "parallel"),
        name="position_dft_stage_a",
    )(xv, t2, twc, tws)
    c1, s1 = _dft_tables(n1)
    tn = 4096
    cols = n2 * w
    blk = pl.BlockSpec((1, n1, tn), lambda bi, j: (bi, 0, j))
    tab = pl.BlockSpec((n1, n1), lambda bi, j: (0, 0))
    pr, pi = pl.pallas_call(
        _fft_stage_c_kernel,
        grid=(b, cols // tn),
        in_specs=[blk, blk, tab, tab],
        out_specs=[blk, blk],
        out_shape=[sds((b, n1, cols), F32)] * 2,
        compiler_params=_params("parallel", "parallel"),
        name="position_dft_stage_c",
    )(ur.reshape(b, n1, cols), ui.reshape(b, n1, cols), jnp.asarray(c1, F32), jnp.asarray(s1, F32))
    return pr.reshape(b, l, w), pi.reshape(b, l, w)


def _direct_dft_kernel(x_ref, t_ref, pr_ref, pi_ref):
    l = x_ref.shape[1]
    u = _dot(t_ref[...].astype(BF16), x_ref[0].astype(BF16))
    pr_ref[0] = u[0:l]
    pi_ref[0] = u[l:2 * l]


def _position_dft_direct(fu):
    b, l, w = fu.shape
    scale = 1.0 / math.sqrt(l * FNET_GROUP_DIM)
    c, s = _dft_tables(l)
    t = jnp.asarray(np.concatenate([c, -s], axis=0) * scale, dtype=F32)
    spec = pl.BlockSpec((1, l, w), lambda bi: (bi, 0, 0))
    return pl.pallas_call(
        _direct_dft_kernel,
        grid=(b,),
        in_specs=[spec, pl.BlockSpec((2 * l, l), lambda bi: (0, 0))],
        out_specs=[spec, spec],
        out_shape=[jax.ShapeDtypeStruct((b, l, w), F32)] * 2,
        compiler_params=_params("parallel"),
        name="context_position_dft",
    )(fu, t)


def _merge_kernel(na_ref, df_ref, pr_ref, pi_ref, x_ref, g1_ref, sh2_ref, sc2_ref, n2g_ref, subg_ref,
                  g64_ref, wc_ref, ws_ref, wf_ref, wo_ref, x1_ref, h2_ref, *, diff_scale):
    d = df_ref[0]
    gms = _dot((d * d).astype(BF16), g64_ref[...])
    dn = d * lax.rsqrt(gms + EPS) * subg_ref[...] * diff_scale
    fr = (_dot(pr_ref[0].astype(BF16), wc_ref[...].astype(BF16))
          + _dot(pi_ref[0].astype(BF16), ws_ref[...].astype(BF16)))
    four = _dot(fr.astype(BF16), wf_ref[...])
    y = (_dot(na_ref[0], wo_ref[0:NA_WIDTH, :])
         + _dot(dn.astype(BF16), wo_ref[NA_WIDTH:NA_WIDTH + DIFF_WIDTH, :])
         + _dot(four.astype(BF16), wo_ref[NA_WIDTH + DIFF_WIDTH:, :]))
    x1 = x_ref[0] + g1_ref[0] * y
    x1_ref[0] = x1
    ms = jnp.mean(x1 * x1, axis=-1, keepdims=True)
    h = x1 * lax.rsqrt(ms + EPS) * n2g_ref[...]
    h2_ref[0] = (h * (1.0 + sc2_ref[0]) + sh2_ref[0]).astype(BF16)


def _channel_dft_tables():
    c, s = _dft_tables(FNET_GROUP_DIM)
    eye = np.eye(FNET_GROUPS)
    return jnp.asarray(np.kron(eye, c), F32), jnp.asarray(np.kron(eye, s), F32)


def _merge(na_o, diff_o, pr, pi, x, g1, sh2, sc2, norm2_g, subln_g, w_four, w_out, lam_init, tm):
    b, s, d = x.shape
    wc, ws = _channel_dft_tables()
    g64 = _group_mean_matrix(DIFF_WIDTH, DIFF_V_DIM)
    const = lambda shape: pl.BlockSpec(shape, lambda bi, i: (0,) * len(shape))
    per_b = pl.BlockSpec((1, 1, d), lambda bi, i: (bi, 0, 0))
    tok = lambda w: pl.BlockSpec((1, tm, w), lambda bi, i: (bi, i, 0))
    sq = (FNET_WIDTH, FNET_WIDTH)
    return pl.pallas_call(
        functools.partial(_merge_kernel, diff_scale=1.0 - lam_init),
        grid=(b, s // tm),
        in_specs=[tok(NA_WIDTH), tok(DIFF_WIDTH), tok(FNET_WIDTH), tok(FNET_WIDTH), tok(d),
                  per_b, per_b, per_b, const((1, d)), const((1, DIFF_WIDTH)),
                  const((DIFF_WIDTH, DIFF_WIDTH)), const(sq), const(sq), const(sq), const((d, d))],
        out_specs=[tok(d), tok(d)],
        out_shape=[jax.ShapeDtypeStruct((b, s, d), F32), jax.ShapeDtypeStruct((b, s, d), BF16)],
        compiler_params=_params("parallel", "parallel"),
        name="merge_projection",
    )(na_o, diff_o, pr, pi, x, g1, sh2, sc2, norm2_g, subln_g, g64, wc, ws, w_four, w_out)


FFN_TN = 256
FFN_HALO = BF16_SUBLANES


def _ffn_kernel(hp_ref, h_ref, hn_ref, wg_ref, wv_ref, cwg_ref, cwv_ref, cbg_ref, cbv_ref, wd_ref,
                x1_ref, g2_ref, o_ref, hcat, acc):
    i = pl.program_id(1)
    j = pl.program_id(2)
    tm = h_ref.shape[1]

    @pl.when(j == 0)
    def _():
        hp = hp_ref[0]
        hn = hn_ref[0]
        hcat[0:FFN_HALO] = jnp.where(i > 0, hp, jnp.zeros_like(hp))
        hcat[FFN_HALO:FFN_HALO + tm] = h_ref[0]
        hcat[FFN_HALO + tm:] = jnp.where(i < pl.num_programs(1) - 1, hn, jnp.zeros_like(hn))
        acc[...] = jnp.zeros(acc.shape, F32)

    hc = hcat[...]

    def conv(w_ref, cw_ref, cb_ref):
        u = _dot(hc, w_ref[...])
        total = u.shape[0]
        prv = pltpu.roll(u, 1, axis=0)[FFN_HALO:FFN_HALO + tm]
        cur = u[FFN_HALO:FFN_HALO + tm]
        nxt = pltpu.roll(u, total - 1, axis=0)[FFN_HALO:FFN_HALO + tm]
        cw = cw_ref[...]
        return prv * cw[0:1] + cur * cw[1:2] + nxt * cw[2:3] + cb_ref[...]

    gate = conv(wg_ref, cwg_ref, cbg_ref)
    val = conv(wv_ref, cwv_ref, cbv_ref)
    act = (gate * _sigmoid(gate) * val).astype(BF16)
    acc[...] += _dot(act, wd_ref[...])

    @pl.when(j == pl.num_programs(2) - 1)
    def _():
        o_ref[0] = x1_ref[0] + g2_ref[0] * acc[...]


def _conv_ffn(h2, x1, g2, w_up, conv_w, conv_b, w_down, tm):
    b, s, d = x1.shape
    d_ff = w_down.shape[0]
    nj = d_ff // FFN_TN
    per_tile = tm // FFN_HALO
    n_halo_blocks = s // FFN_HALO
    return pl.pallas_call(
        _ffn_kernel,
        grid=(b, s // tm, nj),
        in_specs=[
            pl.BlockSpec((1, FFN_HALO, d), lambda bi, i, j: (bi, jnp.maximum(i * per_tile - 1, 0), 0)),
            pl.BlockSpec((1, tm, d), lambda bi, i, j: (bi, i, 0)),
            pl.BlockSpec((1, FFN_HALO, d), lambda bi, i, j: (bi, jnp.minimum((i + 1) * per_tile, n_halo_blocks - 1), 0)),
            pl.BlockSpec((d, FFN_TN), lambda bi, i, j: (0, j)),
            pl.BlockSpec((d, FFN_TN), lambda bi, i, j: (0, nj + j)),
            pl.BlockSpec((3, FFN_TN), lambda bi, i, j: (0, j)),
            pl.BlockSpec((3, FFN_TN), lambda bi, i, j: (0, nj + j)),
            pl.BlockSpec((1, FFN_TN), lambda bi, i, j: (0, j)),
            pl.BlockSpec((1, FFN_TN), lambda bi, i, j: (0, nj + j)),
            pl.BlockSpec((FFN_TN, d), lambda bi, i, j: (j, 0)),
            pl.BlockSpec((1, tm, d), lambda bi, i, j: (bi, i, 0)),
            pl.BlockSpec((1, 1, d), lambda bi, i, j: (bi, 0, 0)),
        ],
        out_specs=pl.BlockSpec((1, tm, d), lambda bi, i, j: (bi, i, 0)),
        out_shape=jax.ShapeDtypeStruct((b, s, d), F32),
        scratch_shapes=[pltpu.VMEM((tm + 2 * FFN_HALO, d), BF16), pltpu.VMEM((tm, d), F32)],
        compiler_params=_params("parallel", "parallel", "arbitrary"),
        name="conv_ffn",
    )(h2, h2, h2, w_up, w_up, conv_w, conv_w, conv_b, conv_b, w_down, x1, g2)


def kernel(x, c, ctx, c_ctx, w_mod, b_mod, norm1_g, w_in, na_q_g, na_k_g, na_rpb, diff_q_g, diff_k_g,
           diff_lambda, diff_subln_g, w_fourier, w_out, norm2_g, w_up, conv_w, conv_b, w_down):
    b, s, d = x.shape
    n_ctx = ctx.shape[1]
    depth = w_mod.shape[0]

    cc = jnp.zeros((8, d), F32).at[0:b].set(c).at[b].set(c_ctx)
    mod = _modulation(cc, w_mod, b_mod)

    cos_x, sin_x = _rope_tables(s)
    cos_c = jnp.ones((n_ctx, DIFF_WIDTH), F32)
    sin_c = jnp.zeros((n_ctx, DIFF_WIDTH), F32)

    cx = ctx
    for l in range(depth):
        lam_init = 0.8 - 0.6 * math.exp(-0.3 * l)
        mx = mod[l, 0:b].reshape(b, 1, 6, d)
        mc = jnp.broadcast_to(mod[l, b].reshape(1, 1, 6, d), (b, 1, 6, d))
        sh1, sc1, g1, sh2, sc2, g2 = (mx[:, :, t] for t in range(6))
        csh1, csc1, cg1, csh2, csc2, cg2 = (mc[:, :, t] for t in range(6))

        w_in_b = w_in[l].astype(BF16)
        w_out_b = w_out[l].astype(BF16)
        w_four_b = w_fourier[l].astype(BF16)
        w_up_b = w_up[l].astype(BF16)
        w_down_b = w_down[l].astype(BF16)
        n1g = norm1_g[l].reshape(1, d)
        n2g = norm2_g[l].reshape(1, d)
        gq = jnp.tile(na_q_g[l], NA_HEADS).reshape(1, NA_WIDTH)
        gk = jnp.tile(na_k_g[l], NA_HEADS).reshape(1, NA_WIDTH)
        dgq = jnp.tile(diff_q_g[l], 2 * DIFF_HEADS).reshape(1, DIFF_WIDTH)
        dgk = jnp.tile(diff_k_g[l], 2 * DIFF_HEADS).reshape(1, DIFF_WIDTH)
        subg = jnp.tile(diff_subln_g[l], DIFF_HEADS).reshape(1, DIFF_WIDTH)
        cb = conv_b[l].reshape(1, -1)

        naq, nak, nav, dqt, dk, dvt, fu = _in_projection(
            x, sh1, sc1, n1g, w_in_b, gq, gk, dgq, dgk, cos_x, sin_x, tm=512)
        cnaq, cnak, cnav, cdqt, cdk, cdvt, cfu = _in_projection(
            cx, csh1, csc1, n1g, w_in_b, gq, gk, dgq, dgk, cos_c, sin_c, tm=n_ctx)

        na_x = _neighbourhood_attention(naq, nak, nav, cnak, cnav, na_rpb[l])
        diff_x = _diff_attention(diff_lambda[l], dqt, dk, dvt, lam_init, tq=128,
                                 ctx_kv=(cdk, cdvt.reshape(b, DIFF_WIDTH, n_ctx)))
        pr, pi = _position_dft(fu)
        x1, h2 = _merge(na_x, diff_x, pr, pi, x, g1, sh2, sc2, n2g, subg, w_four_b, w_out_b, lam_init, tm=512)
        x = _conv_ffn(h2, x1, g2, w_up_b, conv_w[l], cb, w_down_b, tm=1024)

        if l < depth - 1:
            na_c = _dense_na_attention(cnaq, cnak, cnav)
            diff_c = _diff_attention(diff_lambda[l], cdqt, cdk, cdvt, lam_init, tq=128)
            cpr, cpi = _position_dft_direct(cfu)
            cx1, ch2 = _merge(na_c, diff_c, cpr, cpi, cx, cg1, csh2, csc2, n2g, subg, w_four_b, w_out_b,
                              lam_init, tm=n_ctx)
            cx = _conv_ffn(ch2, cx1, cg2, w_up_b, conv_w[l], cb, w_down_b, tm=n_ctx)
    return x
```

```python
import functools
import math

import numpy as np
import jax
import jax.numpy as jnp
from jax import lax
from jax.experimental import pallas as pl
from jax.experimental.pallas import tpu as pltpu

F32 = jnp.float32
BF16 = jnp.bfloat16

GRID_W = 64
EPS = 1e-6
ROPE_BASE = 10000.0
HEAD_DIM = 64
NA_HEADS = 8
NA_WIDTH = NA_HEADS * HEAD_DIM
NA_KH = 8
NA_KW = 16
DIFF_HEADS = 4
DIFF_QK_DIM = 32
DIFF_V_DIM = 64
DIFF_WIDTH = DIFF_HEADS * DIFF_V_DIM
FNET_GROUPS = 4
FNET_GROUP_DIM = 64
FNET_WIDTH = FNET_GROUPS * FNET_GROUP_DIM
NA_Q0 = 0
NA_K0 = NA_Q0 + NA_WIDTH
NA_V0 = NA_K0 + NA_WIDTH
DQ0 = NA_V0 + NA_WIDTH
DK0 = DQ0 + DIFF_WIDTH
DV0 = DK0 + DIFF_WIDTH
FN0 = DV0 + DIFF_WIDTH
IN_WIDTH = FN0 + FNET_WIDTH
NEG_BIG = -1e30
LOG2E = 1.4426950408889634

VMEM_LIMIT_BYTES = 52 * 1024 * 1024
LANES = 128
BF16_SUBLANES = 16


def _params(*sem, flags=None):
    return pltpu.CompilerParams(dimension_semantics=sem, vmem_limit_bytes=VMEM_LIMIT_BYTES, flags=flags)


def _dot(a, b):
    return jnp.dot(a, b, preferred_element_type=F32)


def _dot_nt(a, b):
    return lax.dot_general(a, b, (((1,), (1,)), ((), ())), preferred_element_type=F32)


def _sigmoid(x):
    return 1.0 / (1.0 + jnp.exp(-x))


def _group_mean_matrix(width, group):
    idx = np.arange(width) // group
    return jnp.asarray((idx[:, None] == idx[None, :]).astype(np.float32) / group, dtype=BF16)


def _mod_kernel(c_ref, w_ref, b_ref, o_ref):
    c = c_ref[...]
    s = c * _sigmoid(c)
    s_hi = s.astype(BF16)
    s_lo = (s - s_hi.astype(F32)).astype(BF16)
    w = w_ref[0]
    w_hi = w.astype(BF16)
    w_lo = (w - w_hi.astype(F32)).astype(BF16)
    o_ref[0] = _dot(s_hi, w_hi) + _dot(s_hi, w_lo) + _dot(s_lo, w_hi) + b_ref[0]


def _modulation(cc, w_mod, b_mod):
    depth, d, n = w_mod.shape
    tn = 768
    return pl.pallas_call(
        _mod_kernel,
        grid=(depth, n // tn),
        in_specs=[
            pl.BlockSpec((8, d), lambda l, j: (0, 0)),
            pl.BlockSpec((1, d, tn), lambda l, j: (l, 0, j)),
            pl.BlockSpec((1, 1, tn), lambda l, j: (l, 0, j)),
        ],
        out_specs=pl.BlockSpec((1, 8, tn), lambda l, j: (l, 0, j)),
        out_shape=jax.ShapeDtypeStruct((depth, 8, n), F32),
        compiler_params=_params("parallel", "parallel"),
        name="modulation",
    )(cc, w_mod, b_mod.reshape(depth, 1, n))


def _inproj_kernel(x_ref, sh_ref, sc_ref, g_ref, w_ref, gq_ref, gk_ref, dgq_ref, dgk_ref,
                   g64_ref, g32_ref, cos_ref, sin_ref,
                   naq_ref, nak_ref, nav_ref, dqt_ref, dk_ref, dvt_ref, fu_ref):
    x = x_ref[0]
    ms = jnp.mean(x * x, axis=-1, keepdims=True)
    h = x * lax.rsqrt(ms + EPS) * g_ref[...]
    h = h * (1.0 + sc_ref[0]) + sh_ref[0]
    hb = h.astype(BF16)

    def proj(c0, c1):
        return _dot(hb, w_ref[:, c0:c1])

    def group_norm(p, gmat_ref, gain):
        gms = _dot((p * p).astype(BF16), gmat_ref[...])
        return p * lax.rsqrt(gms + EPS) * gain

    def rope(y):
        lane = lax.broadcasted_iota(jnp.int32, y.shape, 1)
        nxt = pltpu.roll(y, DIFF_WIDTH - 8, axis=1)
        prv = pltpu.roll(y, 8, axis=1)
        partner = jnp.where((lane & 8) == 0, nxt, prv)
        return y * cos_ref[...] + partner * sin_ref[...]

    q = group_norm(proj(NA_Q0, NA_K0), g64_ref, gq_ref[...]) * (HEAD_DIM ** -0.5)
    naq_ref[0] = q.astype(BF16)
    k = group_norm(proj(NA_K0, NA_V0), g64_ref, gk_ref[...])
    nak_ref[0] = k.astype(BF16)
    nav_ref[0] = proj(NA_V0, DQ0).astype(BF16)
    dq = rope(group_norm(proj(DQ0, DK0), g32_ref, dgq_ref[...])) * (DIFF_QK_DIM ** -0.5 * LOG2E)
    dqt_ref[0] = dq.T.astype(BF16)
    dk = rope(group_norm(proj(DK0, DV0), g32_ref, dgk_ref[...]))
    dk_ref[0] = dk.astype(BF16)
    dvt_ref[0, 0] = proj(DV0, FN0).T.astype(BF16)
    fu_ref[0] = proj(FN0, IN_WIDTH)


def _in_projection(x, shift, scale, norm_g, w_in, gq, gk, dgq, dgk, cos_t, sin_t, tm):
    b, s, d = x.shape
    g64 = _group_mean_matrix(NA_WIDTH, HEAD_DIM)
    g32 = _group_mean_matrix(DIFF_WIDTH, DIFF_QK_DIM)
    const = lambda shape: pl.BlockSpec(shape, lambda bi, i: (0,) * len(shape))
    per_b = pl.BlockSpec((1, 1, d), lambda bi, i: (bi, 0, 0))
    tok = lambda w: pl.BlockSpec((1, tm, w), lambda bi, i: (bi, i, 0))
    tok_t = lambda w: pl.BlockSpec((1, w, tm), lambda bi, i: (bi, 0, i))
    sds = jax.ShapeDtypeStruct
    return pl.pallas_call(
        _inproj_kernel,
        grid=(b, s // tm),
        in_specs=[
            tok(d), per_b, per_b, const((1, d)), const((d, IN_WIDTH)),
            const((1, NA_WIDTH)), const((1, NA_WIDTH)), const((1, DIFF_WIDTH)), const((1, DIFF_WIDTH)),
            const((NA_WIDTH, NA_WIDTH)), const((DIFF_WIDTH, DIFF_WIDTH)),
            pl.BlockSpec((tm, DIFF_WIDTH), lambda bi, i: (i, 0)),
            pl.BlockSpec((tm, DIFF_WIDTH), lambda bi, i: (i, 0)),
        ],
        out_specs=[tok(NA_WIDTH), tok(NA_WIDTH), tok(NA_WIDTH),
                   tok_t(DIFF_WIDTH), tok(DIFF_WIDTH),
                   pl.BlockSpec((1, 1, DIFF_WIDTH, tm), lambda bi, i: (bi, i, 0, 0)), tok(FNET_WIDTH)],
        out_shape=[sds((b, s, NA_WIDTH), BF16), sds((b, s, NA_WIDTH), BF16), sds((b, s, NA_WIDTH), BF16),
                   sds((b, DIFF_WIDTH, s), BF16), sds((b, s, DIFF_WIDTH), BF16),
                   sds((b, s // tm, DIFF_WIDTH, tm), BF16), sds((b, s, FNET_WIDTH), F32)],
        compiler_params=_params("parallel", "parallel"),
        name="in_projection",
    )(x, shift, scale, norm_g, w_in, gq, gk, dgq, dgk, g64, g32, cos_t, sin_t)


def _rope_tables(s):
    pos = jnp.arange(s)
    rows = (pos // GRID_W).astype(F32)
    cols = (pos % GRID_W).astype(F32)
    m = DIFF_QK_DIM // 2
    inv = ROPE_BASE ** (-jnp.arange(0, m, 2, dtype=F32) / m)
    ang_r = rows[:, None] * inv[None, :]
    ang_c = cols[:, None] * inv[None, :]
    ang = jnp.concatenate([ang_r, ang_r, ang_c, ang_c], axis=1)
    sign = jnp.asarray(np.tile(np.repeat([-1.0, 1.0], 8), 2), F32)
    cos32 = jnp.cos(ang)
    sin32 = jnp.sin(ang) * sign[None, :]
    reps = DIFF_WIDTH // DIFF_QK_DIM
    return jnp.tile(cos32, (1, reps)), jnp.tile(sin32, (1, reps))


NA_GROUP_ROWS = 8
NA_HALF = NA_WIDTH // 2
NA_HEADS_PER_HALF = NA_HALF // HEAD_DIM


def _head_block_mask(rows_per_head, n_heads, width):
    r = lax.broadcasted_iota(jnp.int32, (rows_per_head * n_heads, width), 0) // rows_per_head
    c = lax.broadcasted_iota(jnp.int32, (rows_per_head * n_heads, width), 1) // (width // n_heads)
    return r == c


def _na_kernel(q_ref, kp_ref, k0_ref, kn_ref, vp_ref, v0_ref, vn_ref, kc_ref, vc_ref, bias_ref,
               o_ref, kbuf, vbuf, s_a, s_b):
    g = pl.program_id(1)
    blk = NA_GROUP_ROWS * GRID_W
    kbuf[0:blk] = kp_ref[0]
    kbuf[blk:2 * blk] = k0_ref[0]
    kbuf[2 * blk:3 * blk] = kn_ref[0]
    vbuf[0:blk] = vp_ref[0]
    vbuf[blk:2 * blk] = v0_ref[0]
    vbuf[2 * blk:3 * blk] = vn_ref[0]
    n_rows = pl.num_programs(1) * NA_GROUP_ROWS
    head_mask = _head_block_mask(GRID_W, NA_HEADS_PER_HALF, NA_HALF)
    n_keys = NA_KH * GRID_W

    s_bufs = (s_a, s_b)

    def window(i):
        r = g * NA_GROUP_ROWS + i
        row_start = jnp.clip(r - NA_KH // 2, 0, n_rows - NA_KH)
        off = pl.multiple_of((row_start - g * NA_GROUP_ROWS + NA_GROUP_ROWS) * GRID_W, GRID_W)
        return off, row_start - r + (NA_KH - 1)

    def scores(t, s_buf):
        i, hh = divmod(t, 2)
        off, d0 = window(i)
        cs = slice(hh * NA_HALF, (hh + 1) * NA_HALF)
        qh = q_ref[0, i * GRID_W:(i + 1) * GRID_W, cs]
        qst = jnp.concatenate([qh] * NA_HEADS_PER_HALF, axis=0)
        qst = jnp.where(head_mask, qst, jnp.zeros_like(qst))
        bias = jnp.concatenate([bias_ref[d0 + 2 * j, hh] for j in range(NA_KH // 2)], axis=1)
        s_buf[:, 0:n_keys] = _dot_nt(qst, kbuf[pl.ds(off, n_keys), cs]) + bias
        s_buf[:, n_keys:] = _dot_nt(qst, kc_ref[0, :, cs])

    def finish(t, s_buf):
        i, hh = divmod(t, 2)
        off, _ = window(i)
        cs = slice(hh * NA_HALF, (hh + 1) * NA_HALF)
        s = s_buf[...]
        m = jnp.max(s, axis=-1, keepdims=True)
        e = jnp.exp(s - m)
        l = jnp.sum(e, axis=-1, keepdims=True)
        eb = e.astype(BF16)
        o = _dot(eb[:, 0:n_keys], vbuf[pl.ds(off, n_keys), cs]) + _dot(eb[:, n_keys:], vc_ref[0, :, cs])
        o = jnp.where(head_mask, o * (1.0 / l), 0.0)
        out = o[0:GRID_W]
        for hl in range(1, NA_HEADS_PER_HALF):
            out = out + o[hl * GRID_W:(hl + 1) * GRID_W]
        o_ref[0, i * GRID_W:(i + 1) * GRID_W, cs] = out.astype(o_ref.dtype)

    n_jobs = 2 * NA_GROUP_ROWS
    scores(0, s_a)
    for t in range(n_jobs):
        if t + 1 < n_jobs:
            scores(t + 1, s_bufs[(t + 1) % 2])
        finish(t, s_bufs[t % 2])


def _na_bias_table(rpb):
    h, n_dr, _ = rpb.shape
    qc = np.arange(GRID_W)
    kc = np.arange(GRID_W)
    col_start = np.clip(qc - NA_KW // 2, 0, GRID_W - NA_KW)
    valid = (kc[None, :] >= col_start[:, None]) & (kc[None, :] < col_start[:, None] + NA_KW)
    pad = GRID_W - NA_KW
    period = 2 * GRID_W - 1
    rext = jnp.pad(rpb, ((0, 0), (0, 0), (pad, pad)))
    flat = jnp.tile(rext, (1, 1, GRID_W + 1))[:, :, :GRID_W * (period + 1)]
    hankel = flat.reshape(h, n_dr, GRID_W, period + 1)[..., :GRID_W]
    toep = hankel[:, :, ::-1, :]
    toep = jnp.where(jnp.asarray(valid)[None, None], toep, NEG_BIG)
    pair = jnp.concatenate([toep[:, :-1], toep[:, 1:]], axis=-1)
    pair = pair.reshape(2, NA_HEADS_PER_HALF, n_dr - 1, GRID_W, 2 * GRID_W).transpose(2, 0, 1, 3, 4)
    return pair.reshape(n_dr - 1, 2, NA_HEADS_PER_HALF * GRID_W, 2 * GRID_W).astype(F32)


def _neighbourhood_attention(q, k, v, kc, vc, rpb):
    b, s, _ = q.shape
    n_groups = s // (NA_GROUP_ROWS * GRID_W)
    blk = NA_GROUP_ROWS * GRID_W
    c = kc.shape[1]
    bias = _na_bias_table(rpb)
    cur = pl.BlockSpec((1, blk, NA_WIDTH), lambda bi, g: (bi, g, 0))
    prv = pl.BlockSpec((1, blk, NA_WIDTH), lambda bi, g: (bi, jnp.maximum(g - 1, 0), 0))
    nxt = pl.BlockSpec((1, blk, NA_WIDTH), lambda bi, g: (bi, jnp.minimum(g + 1, n_groups - 1), 0))
    ctx = pl.BlockSpec((1, c, NA_WIDTH), lambda bi, g: (bi, 0, 0))
    return pl.pallas_call(
        _na_kernel,
        grid=(b, n_groups),
        in_specs=[cur, prv, cur, nxt, prv, cur, nxt, ctx, ctx,
                  pl.BlockSpec(bias.shape, lambda bi, g: (0, 0, 0, 0))],
        out_specs=cur,
        out_shape=jax.ShapeDtypeStruct((b, s, NA_WIDTH), BF16),
        scratch_shapes=[pltpu.VMEM((3 * blk, NA_WIDTH), BF16), pltpu.VMEM((3 * blk, NA_WIDTH), BF16)]
                       + [pltpu.VMEM((NA_HEADS_PER_HALF * GRID_W, NA_KH * GRID_W + c), F32)] * 2,
        compiler_params=_params("parallel", "arbitrary"),
        name="neighbourhood_attention",
    )(q, k, k, k, v, v, v, kc, vc, bias)


def _dense_na_kernel(q_ref, k_ref, v_ref, o_ref):
    c = q_ref.shape[1]
    head_mask = _head_block_mask(c, NA_HEADS_PER_HALF, NA_HALF)
    for hh in range(2):
        cs = slice(hh * NA_HALF, (hh + 1) * NA_HALF)
        qh = q_ref[0, :, cs]
        qst = jnp.concatenate([qh] * NA_HEADS_PER_HALF, axis=0)
        qst = jnp.where(head_mask, qst, jnp.zeros_like(qst))
        s = _dot_nt(qst, k_ref[0, :, cs])
        m = jnp.max(s, axis=-1, keepdims=True)
        e = jnp.exp(s - m)
        l = jnp.sum(e, axis=-1, keepdims=True)
        o = _dot(e.astype(BF16), v_ref[0, :, cs])
        o = jnp.where(head_mask, o * (1.0 / l), 0.0)
        out = o[0:c]
        for hl in range(1, NA_HEADS_PER_HALF):
            out = out + o[hl * c:(hl + 1) * c]
        o_ref[0, :, cs] = out.astype(o_ref.dtype)


def _dense_na_attention(q, k, v):
    b, c, w = q.shape
    spec = pl.BlockSpec((1, c, w), lambda bi: (bi, 0, 0))
    return pl.pallas_call(
        _dense_na_kernel,
        grid=(b,),
        in_specs=[spec, spec, spec],
        out_specs=spec,
        out_shape=jax.ShapeDtypeStruct((b, c, w), BF16),
        compiler_params=_params("parallel"),
        name="context_dense_attention",
    )(q, k, v)


N_DIFF_STREAMS = 2 * DIFF_HEADS


DIFF_HEAD_PAIRS = DIFF_HEADS // 2


def _diff_kernel(*refs, lam_init, has_ctx):
    n_in = 6 if has_ctx else 4
    lam_ref, qt_ref, k_ref, vt_ref = refs[:4]
    kc_ref, vct_ref = refs[4:6] if has_ctx else (None, None)
    o_ref, qs, s_a, s_b, m_scr, l_scr, acc = refs[n_in:]
    s_bufs = (s_a, s_b)
    tq = qt_ref.shape[2]
    n_chunks, _, tk = vt_ref.shape[1:]
    pair_w = 4 * tq

    qt = qt_ref[0]
    stream = lax.broadcasted_iota(jnp.int32, qt.shape, 0) // DIFF_QK_DIM
    for st in range(N_DIFF_STREAMS):
        qs[:, st * tq:(st + 1) * tq] = jnp.where(stream == st, qt, jnp.zeros_like(qt))
    m_scr[...] = jnp.full(m_scr.shape, NEG_BIG, F32)
    l_scr[...] = jnp.zeros(l_scr.shape, F32)
    acc[...] = jnp.zeros(acc.shape, F32)

    def scores(kk, s_buf):
        n = kk.shape[0]
        for pair in range(DIFF_HEAD_PAIRS):
            cols = slice(pair * pair_w, (pair + 1) * pair_w)
            s_buf[0:n, cols] = _dot(kk, qs[:, cols])

    def absorb(s_buf, n, vt_rows):
        for pair in range(DIFF_HEAD_PAIRS):
            cols = slice(pair * pair_w, (pair + 1) * pair_w)
            s = s_buf[0:n, cols]
            m_old = m_scr[:, cols]
            m_new = jnp.maximum(m_old, jnp.max(s, axis=0, keepdims=True))
            alpha = jnp.exp2(m_old - m_new)
            p = jnp.exp2(s - m_new)
            l_scr[:, cols] = alpha * l_scr[:, cols] + jnp.sum(p, axis=0, keepdims=True)
            m_scr[:, cols] = m_new
            pb = p.astype(BF16)
            for hh in range(2):
                h = 2 * pair + hh
                rows = slice(h * DIFF_V_DIM, (h + 1) * DIFF_V_DIM)
                hc = slice(hh * 2 * tq, (hh + 1) * 2 * tq)
                acc[rows, :] = acc[rows, :] * alpha[:, hc] + _dot(vt_rows(rows), pb[:, hc])

    n_items = n_chunks + (1 if has_ctx else 0)

    def scores_item(t, buf):
        if has_ctx and isinstance(t, int) and t == n_chunks:
            scores(kc_ref[0], buf)
        else:
            start = t * tk if isinstance(t, int) else pl.multiple_of(t * tk, tk)
            scores(k_ref[0, pl.ds(start, tk), :], buf)

    def absorb_item(t, buf):
        if has_ctx and isinstance(t, int) and t == n_chunks:
            absorb(buf, kc_ref.shape[1], lambda rows: vct_ref[0, rows, :])
        else:
            absorb(buf, tk, lambda rows: vt_ref[0, t, rows, :])

    scores_item(0, s_a)
    n_pairs = (n_chunks - 1) // 2

    def body(i, carry):
        t = 2 * i
        scores_item(t + 1, s_b)
        absorb_item(t, s_a)
        scores_item(t + 2, s_a)
        absorb_item(t + 1, s_b)
        return carry

    lax.fori_loop(0, n_pairs, body, 0)
    for t in range(2 * n_pairs, n_items):
        if t + 1 < n_items:
            scores_item(t + 1, s_bufs[(t + 1) % 2])
        absorb_item(t, s_bufs[t % 2])

    lp = lam_ref[...]
    lam = (jnp.exp(jnp.sum(lp[0:1] * lp[1:2], axis=1, keepdims=True))
           - jnp.exp(jnp.sum(lp[2:3] * lp[3:4], axis=1, keepdims=True)) + lam_init)
    inv_l = 1.0 / l_scr[...]
    outs = []
    for h in range(DIFF_HEADS):
        rows = slice(h * DIFF_V_DIM, (h + 1) * DIFF_V_DIM)
        o1 = acc[rows, 0:tq] * inv_l[:, 2 * h * tq:(2 * h + 1) * tq]
        o2 = acc[rows, tq:2 * tq] * inv_l[:, (2 * h + 1) * tq:(2 * h + 2) * tq]
        outs.append(o1 - lam * o2)
    o_ref[0] = jnp.concatenate(outs, axis=0).T


def _diff_attention(lam_p, qt, k, vt, lam_init, tq, ctx_kv=None):
    b, w, sq = qt.shape
    sk = k.shape[1]
    n_chunks, _, tk = vt.shape[1:]
    whole = lambda a: pl.BlockSpec((1,) + a.shape[1:], lambda bi, i: (bi,) + (0,) * (a.ndim - 1))
    operands = [lam_p, qt, k, vt]
    in_specs = [
        pl.BlockSpec(lam_p.shape, lambda bi, i: (0, 0)),
        pl.BlockSpec((1, w, tq), lambda bi, i: (bi, 0, i)),
        whole(k), whole(vt),
    ]
    if ctx_kv is not None:
        operands += list(ctx_kv)
        in_specs += [whole(ctx_kv[0]), whole(ctx_kv[1])]
    return pl.pallas_call(
        functools.partial(_diff_kernel, lam_init=lam_init, has_ctx=ctx_kv is not None),
        grid=(b, sq // tq),
        in_specs=in_specs,
        out_specs=pl.BlockSpec((1, tq, w), lambda bi, i: (bi, i, 0)),
        out_shape=jax.ShapeDtypeStruct((b, sq, w), F32),
        scratch_shapes=[
            pltpu.VMEM((w, N_DIFF_STREAMS * tq), BF16),
            pltpu.VMEM((tk, N_DIFF_STREAMS * tq), F32),
            pltpu.VMEM((tk, N_DIFF_STREAMS * tq), F32),
            pltpu.VMEM((1, N_DIFF_STREAMS * tq), F32),
            pltpu.VMEM((1, N_DIFF_STREAMS * tq), F32),
            pltpu.VMEM((w, 2 * tq), F32),
        ],
        compiler_params=_params("parallel", "arbitrary"),
        name="differential_attention",
    )(*operands)


FFT_N1 = 64
FFT_N1_STEP = 8


def _dft_tables(n):
    idx = np.arange(n)
    ang = 2.0 * np.pi * ((idx[:, None] * idx[None, :]) % n) / n
    return np.cos(ang), np.sin(ang)


def _fft_stage_a_kernel(x_ref, t_ref, twc_ref, tws_ref, ur_ref, ui_ref):
    n2 = x_ref.shape[1]
    t = t_ref[...].astype(BF16)
    for i in range(FFT_N1_STEP):
        xs = x_ref[0, :, i * FNET_WIDTH:(i + 1) * FNET_WIDTH].astype(BF16)
        u = _dot(t, xs)
        ur, ui = u[0:n2], u[n2:2 * n2]
        c = jnp.concatenate([twc_ref[i]] * (FNET_WIDTH // LANES), axis=1)
        s = jnp.concatenate([tws_ref[i]] * (FNET_WIDTH // LANES), axis=1)
        ur_ref[0, i] = ur * c + ui * s
        ui_ref[0, i] = ui * c - ur * s


def _fft_stage_c_kernel(ur_ref, ui_ref, c_ref, s_ref, pr_ref, pi_ref):
    ur = ur_ref[0].astype(BF16)
    ui = ui_ref[0].astype(BF16)
    c1 = c_ref[...].astype(BF16)
    s1 = s_ref[...].astype(BF16)
    pr_ref[0] = _dot(c1, ur) + _dot(s1, ui)
    pi_ref[0] = _dot(c1, ui) - _dot(s1, ur)


def _position_dft(fu):
    b, l, w = fu.shape
    n1, n2 = FFT_N1, l // FFT_N1
    scale = 1.0 / math.sqrt(l * FNET_GROUP_DIM)
    c2, s2 = _dft_tables(n2)
    t2 = jnp.asarray(np.concatenate([c2, -s2], axis=0) * scale, dtype=F32)
    ang = (2.0 * jnp.pi / l) * (jnp.arange(n1, dtype=F32)[:, None] * jnp.arange(n2, dtype=F32)[None, :])
    twc = jnp.broadcast_to(jnp.cos(ang)[:, :, None], (n1, n2, LANES))
    tws = jnp.broadcast_to(jnp.sin(ang)[:, :, None], (n1, n2, LANES))
    xv = fu.reshape(b, n2, n1 * w)
    sds = jax.ShapeDtypeStruct
    ur, ui = pl.pallas_call(
        _fft_stage_a_kernel,
        grid=(b, n1 // FFT_N1_STEP),
        in_specs=[
            pl.BlockSpec((1, n2, FFT_N1_STEP * w), lambda bi, j: (bi, 0, j)),
            pl.BlockSpec((2 * n2, n2), lambda bi, j: (0, 0)),
            pl.BlockSpec((FFT_N1_STEP, n2, LANES), lambda bi, j: (j, 0, 0)),
            pl.BlockSpec((FFT_N1_STEP, n2, LANES), lambda bi, j: (j, 0, 0)),
        ],
        out_specs=[pl.BlockSpec((1, FFT_N1_STEP, n2, w), lambda bi, j: (bi, j, 0, 0))] * 2,
        out_shape=[sds((b, n1, n2, w), F32)] * 2,
        compiler_params=_params("parallel", "parallel"),
        name="position_dft_stage_a",
    )(xv, t2, twc, tws)
    c1, s1 = _dft_tables(n1)
    tn = 4096
    cols = n2 * w
    blk = pl.BlockSpec((1, n1, tn), lambda bi, j: (bi, 0, j))
    tab = pl.BlockSpec((n1, n1), lambda bi, j: (0, 0))
    pr, pi = pl.pallas_call(
        _fft_stage_c_kernel,
        grid=(b, cols // tn),
        in_specs=[blk, blk, tab, tab],
        out_specs=[blk, blk],
        out_shape=[sds((b, n1, cols), F32)] * 2,
        compiler_params=_params("parallel", "parallel"),
        name="position_dft_stage_c",
    )(ur.reshape(b, n1, cols), ui.reshape(b, n1, cols), jnp.asarray(c1, F32), jnp.asarray(s1, F32))
    return pr.reshape(b, l, w), pi.reshape(b, l, w)


def _direct_dft_kernel(x_ref, t_ref, pr_ref, pi_ref):
    l = x_ref.shape[1]
    u = _dot(t_ref[...].astype(BF16), x_ref[0].astype(BF16))
    pr_ref[0] = u[0:l]
    pi_ref[0] = u[l:2 * l]


def _position_dft_direct(fu):
    b, l, w = fu.shape
    scale = 1.0 / math.sqrt(l * FNET_GROUP_DIM)
    c, s = _dft_tables(l)
    t = jnp.asarray(np.concatenate([c, -s], axis=0) * scale, dtype=F32)
    spec = pl.BlockSpec((1, l, w), lambda bi: (bi, 0, 0))
    return pl.pallas_call(
        _direct_dft_kernel,
        grid=(b,),
        in_specs=[spec, pl.BlockSpec((2 * l, l), lambda bi: (0, 0))],
        out_specs=[spec, spec],
        out_shape=[jax.ShapeDtypeStruct((b, l, w), F32)] * 2,
        compiler_params=_params("parallel"),
        name="context_position_dft",
    )(fu, t)


def _merge_kernel(na_ref, df_ref, pr_ref, pi_ref, x_ref, g1_ref, sh2_ref, sc2_ref, n2g_ref, subg_ref,
                  g64_ref, wc_ref, ws_ref, wf_ref, wo_ref, x1_ref, h2_ref, *, diff_scale):
    d = df_ref[0]
    gms = _dot((d * d).astype(BF16), g64_ref[...])
    dn = d * lax.rsqrt(gms + EPS) * subg_ref[...] * diff_scale
    fr = (_dot(pr_ref[0].astype(BF16), wc_ref[...].astype(BF16))
          + _dot(pi_ref[0].astype(BF16), ws_ref[...].astype(BF16)))
    four = _dot(fr.astype(BF16), wf_ref[...])
    y = (_dot(na_ref[0], wo_ref[0:NA_WIDTH, :])
         + _dot(dn.astype(BF16), wo_ref[NA_WIDTH:NA_WIDTH + DIFF_WIDTH, :])
         + _dot(four.astype(BF16), wo_ref[NA_WIDTH + DIFF_WIDTH:, :]))
    x1 = x_ref[0] + g1_ref[0] * y
    x1_ref[0] = x1
    ms = jnp.mean(x1 * x1, axis=-1, keepdims=True)
    h = x1 * lax.rsqrt(ms + EPS) * n2g_ref[...]
    h2_ref[0] = (h * (1.0 + sc2_ref[0]) + sh2_ref[0]).astype(BF16)


def _channel_dft_tables():
    c, s = _dft_tables(FNET_GROUP_DIM)
    eye = np.eye(FNET_GROUPS)
    return jnp.asarray(np.kron(eye, c), F32), jnp.asarray(np.kron(eye, s), F32)


def _merge(na_o, diff_o, pr, pi, x, g1, sh2, sc2, norm2_g, subln_g, w_four, w_out, lam_init, tm):
    b, s, d = x.shape
    wc, ws = _channel_dft_tables()
    g64 = _group_mean_matrix(DIFF_WIDTH, DIFF_V_DIM)
    const = lambda shape: pl.BlockSpec(shape, lambda bi, i: (0,) * len(shape))
    per_b = pl.BlockSpec((1, 1, d), lambda bi, i: (bi, 0, 0))
    tok = lambda w: pl.BlockSpec((1, tm, w), lambda bi, i: (bi, i, 0))
    sq = (FNET_WIDTH, FNET_WIDTH)
    return pl.pallas_call(
        functools.partial(_merge_kernel, diff_scale=1.0 - lam_init),
        grid=(b, s // tm),
        in_specs=[tok(NA_WIDTH), tok(DIFF_WIDTH), tok(FNET_WIDTH), tok(FNET_WIDTH), tok(d),
                  per_b, per_b, per_b, const((1, d)), const((1, DIFF_WIDTH)),
                  const((DIFF_WIDTH, DIFF_WIDTH)), const(sq), const(sq), const(sq), const((d, d))],
        out_specs=[tok(d), tok(d)],
        out_shape=[jax.ShapeDtypeStruct((b, s, d), F32), jax.ShapeDtypeStruct((b, s, d), BF16)],
        compiler_params=_params("parallel", "parallel"),
        name="merge_projection",
    )(na_o, diff_o, pr, pi, x, g1, sh2, sc2, norm2_g, subln_g, g64, wc, ws, w_four, w_out)


FFN_TN = 256
FFN_HALO = BF16_SUBLANES


def _ffn_kernel(hp_ref, h_ref, hn_ref, wu_ref, cw_ref, cb_ref, wd_ref, x1_ref, g2_ref, o_ref,
                hcat, ug_a, uv_a, ug_b, uv_b, a_a, a_b, acc):
    i = pl.program_id(1)
    tm = h_ref.shape[1]
    nj = wd_ref.shape[0]
    hp = hp_ref[0]
    hn = hn_ref[0]
    hcat[0:FFN_HALO] = jnp.where(i > 0, hp, jnp.zeros_like(hp))
    hcat[FFN_HALO:FFN_HALO + tm] = h_ref[0]
    hcat[FFN_HALO + tm:] = jnp.where(i < pl.num_programs(1) - 1, hn, jnp.zeros_like(hn))
    acc[...] = jnp.zeros(acc.shape, F32)
    bufs = ((ug_a, uv_a), (ug_b, uv_b))
    a_bufs = (a_a, a_b)

    def project_up(j, buf):
        buf[0][...] = _dot(hcat[...], wu_ref[j])
        buf[1][...] = _dot(hcat[...], wu_ref[nj + j])

    def conv(u, cw, cb):
        total = u.shape[0]
        prv = pltpu.roll(u, 1, axis=0)[FFN_HALO:FFN_HALO + tm]
        cur = u[FFN_HALO:FFN_HALO + tm]
        nxt = pltpu.roll(u, total - 1, axis=0)[FFN_HALO:FFN_HALO + tm]
        return prv * cw[0:1] + cur * cw[1:2] + nxt * cw[2:3] + cb

    def gate(j, buf, a_buf):
        g = conv(buf[0][...], cw_ref[j], cb_ref[j])
        val = conv(buf[1][...], cw_ref[nj + j], cb_ref[nj + j])
        a_buf[...] = (g * _sigmoid(g) * val).astype(BF16)

    def step(t, par, do_up, do_gate, do_down):
        if do_up:
            project_up(t + 1, bufs[1 - par])
        if do_gate:
            gate(t, bufs[par], a_bufs[par])
        if do_down:
            acc[...] += _dot(a_bufs[1 - par][...], wd_ref[t - 1])

    project_up(0, bufs[0])
    for t in range(nj + 1):
        step(t, t % 2, t + 1 < nj, t < nj, t > 0)
    o_ref[0] = x1_ref[0] + g2_ref[0] * acc[...]


def _conv_ffn(h2, x1, g2, w_up, conv_w, conv_b, w_down, tm):
    b, s, d = x1.shape
    nj = w_down.shape[0]
    per_tile = tm // FFN_HALO
    n_halo_blocks = s // FFN_HALO
    rows = tm + 2 * FFN_HALO
    resident = lambda a: pl.BlockSpec(a.shape, lambda bi, i: (0,) * a.ndim, pipeline_mode=pl.Buffered(1))
    return pl.pallas_call(
        _ffn_kernel,
        grid=(b, s // tm),
        in_specs=[
            pl.BlockSpec((1, FFN_HALO, d), lambda bi, i: (bi, jnp.maximum(i * per_tile - 1, 0), 0)),
            pl.BlockSpec((1, tm, d), lambda bi, i: (bi, i, 0)),
            pl.BlockSpec((1, FFN_HALO, d), lambda bi, i: (bi, jnp.minimum((i + 1) * per_tile, n_halo_blocks - 1), 0)),
            resident(w_up), resident(conv_w), resident(conv_b), resident(w_down),
            pl.BlockSpec((1, tm, d), lambda bi, i: (bi, i, 0)),
            pl.BlockSpec((1, 1, d), lambda bi, i: (bi, 0, 0)),
        ],
        out_specs=pl.BlockSpec((1, tm, d), lambda bi, i: (bi, i, 0)),
        out_shape=jax.ShapeDtypeStruct((b, s, d), F32),
        scratch_shapes=[pltpu.VMEM((rows, d), BF16)] + [pltpu.VMEM((rows, FFN_TN), F32)] * 4
                       + [pltpu.VMEM((tm, FFN_TN), BF16)] * 2 + [pltpu.VMEM((tm, d), F32)],
        compiler_params=_params("parallel", "arbitrary"),
        name="conv_ffn",
    )(h2, h2, h2, w_up, conv_w, conv_b, w_down, x1, g2)


def kernel(x, c, ctx, c_ctx, w_mod, b_mod, norm1_g, w_in, na_q_g, na_k_g, na_rpb, diff_q_g, diff_k_g,
           diff_lambda, diff_subln_g, w_fourier, w_out, norm2_g, w_up, conv_w, conv_b, w_down):
    b, s, d = x.shape
    n_ctx = ctx.shape[1]
    depth = w_mod.shape[0]

    cc = jnp.zeros((8, d), F32).at[0:b].set(c).at[b].set(c_ctx)
    mod = _modulation(cc, w_mod, b_mod)

    cos_x, sin_x = _rope_tables(s)
    cos_c = jnp.ones((n_ctx, DIFF_WIDTH), F32)
    sin_c = jnp.zeros((n_ctx, DIFF_WIDTH), F32)

    cx = ctx
    for l in range(depth):
        lam_init = 0.8 - 0.6 * math.exp(-0.3 * l)
        mx = mod[l, 0:b].reshape(b, 1, 6, d)
        mc = jnp.broadcast_to(mod[l, b].reshape(1, 1, 6, d), (b, 1, 6, d))
        sh1, sc1, g1, sh2, sc2, g2 = (mx[:, :, t] for t in range(6))
        csh1, csc1, cg1, csh2, csc2, cg2 = (mc[:, :, t] for t in range(6))

        w_in_b = w_in[l].astype(BF16)
        w_out_b = w_out[l].astype(BF16)
        w_four_b = w_fourier[l].astype(BF16)
        n_up = w_up.shape[2] // FFN_TN
        w_up_b = w_up[l].astype(BF16).reshape(d, n_up, FFN_TN).transpose(1, 0, 2)
        w_down_b = w_down[l].astype(BF16).reshape(n_up // 2, FFN_TN, d)
        cw = conv_w[l].reshape(3, n_up, FFN_TN).transpose(1, 0, 2)
        n1g = norm1_g[l].reshape(1, d)
        n2g = norm2_g[l].reshape(1, d)
        gq = jnp.tile(na_q_g[l], NA_HEADS).reshape(1, NA_WIDTH)
        gk = jnp.tile(na_k_g[l], NA_HEADS).reshape(1, NA_WIDTH)
        dgq = jnp.tile(diff_q_g[l], 2 * DIFF_HEADS).reshape(1, DIFF_WIDTH)
        dgk = jnp.tile(diff_k_g[l], 2 * DIFF_HEADS).reshape(1, DIFF_WIDTH)
        subg = jnp.tile(diff_subln_g[l], DIFF_HEADS).reshape(1, DIFF_WIDTH)
        cb = conv_b[l].reshape(n_up, 1, FFN_TN)

        naq, nak, nav, dqt, dk, dvt, fu = _in_projection(
            x, sh1, sc1, n1g, w_in_b, gq, gk, dgq, dgk, cos_x, sin_x, tm=512)
        cnaq, cnak, cnav, cdqt, cdk, cdvt, cfu = _in_projection(
            cx, csh1, csc1, n1g, w_in_b, gq, gk, dgq, dgk, cos_c, sin_c, tm=n_ctx)

        na_x = _neighbourhood_attention(naq, nak, nav, cnak, cnav, na_rpb[l])
        diff_x = _diff_attention(diff_lambda[l], dqt, dk, dvt, lam_init, tq=128,
                                 ctx_kv=(cdk, cdvt.reshape(b, DIFF_WIDTH, n_ctx)))
        pr, pi = _position_dft(fu)
        x1, h2 = _merge(na_x, diff_x, pr, pi, x, g1, sh2, sc2, n2g, subg, w_four_b, w_out_b, lam_init, tm=512)
        x = _conv_ffn(h2, x1, g2, w_up_b, cw, cb, w_down_b, tm=512)

        if l < depth - 1:
            na_c = _dense_na_attention(cnaq, cnak, cnav)
            diff_c = _diff_attention(diff_lambda[l], cdqt, cdk, cdvt, lam_init, tq=128)
            cpr, cpi = _position_dft_direct(cfu)
            cx1, ch2 = _merge(na_c, diff_c, cpr, cpi, cx, cg1, csh2, csc2, n2g, subg, w_four_b, w_out_b,
                              lam_init, tm=n_ctx)
            cx = _conv_ffn(ch2, cx1, cg2, w_up_b, cw, cb, w_down_b, tm=n_ctx)
    return x
```

```python
import functools
import math

import numpy as np
import jax
import jax.numpy as jnp
from jax import lax
from jax.experimental import pallas as pl
from jax.experimental.pallas import tpu as pltpu

F32 = jnp.float32
BF16 = jnp.bfloat16

GRID_W = 64
EPS = 1e-6
ROPE_BASE = 10000.0
HEAD_DIM = 64
NA_HEADS = 8
NA_WIDTH = NA_HEADS * HEAD_DIM
NA_KH = 8
NA_KW = 16
DIFF_HEADS = 4
DIFF_QK_DIM = 32
DIFF_V_DIM = 64
DIFF_WIDTH = DIFF_HEADS * DIFF_V_DIM
FNET_GROUPS = 4
FNET_GROUP_DIM = 64
FNET_WIDTH = FNET_GROUPS * FNET_GROUP_DIM
NA_Q0 = 0
NA_K0 = NA_Q0 + NA_WIDTH
NA_V0 = NA_K0 + NA_WIDTH
DQ0 = NA_V0 + NA_WIDTH
DK0 = DQ0 + DIFF_WIDTH
DV0 = DK0 + DIFF_WIDTH
FN0 = DV0 + DIFF_WIDTH
IN_WIDTH = FN0 + FNET_WIDTH
NEG_BIG = -1e30
LOG2E = 1.4426950408889634

VMEM_LIMIT_BYTES = 52 * 1024 * 1024
LANES = 128
BF16_SUBLANES = 16


def _params(*sem, flags=None):
    return pltpu.CompilerParams(dimension_semantics=sem, vmem_limit_bytes=VMEM_LIMIT_BYTES, flags=flags)


def _dot(a, b):
    return jnp.dot(a, b, preferred_element_type=F32)


def _dot_nt(a, b):
    return lax.dot_general(a, b, (((1,), (1,)), ((), ())), preferred_element_type=F32)


def _sigmoid(x):
    return 1.0 / (1.0 + jnp.exp(-x))


def _group_mean_matrix(width, group):
    idx = np.arange(width) // group
    return jnp.asarray((idx[:, None] == idx[None, :]).astype(np.float32) / group, dtype=BF16)


def _mod_kernel(c_ref, w_ref, b_ref, o_ref):
    c = c_ref[...]
    s = c * _sigmoid(c)
    s_hi = s.astype(BF16)
    s_lo = (s - s_hi.astype(F32)).astype(BF16)
    w = w_ref[0]
    w_hi = w.astype(BF16)
    w_lo = (w - w_hi.astype(F32)).astype(BF16)
    o_ref[0] = _dot(s_hi, w_hi) + _dot(s_hi, w_lo) + _dot(s_lo, w_hi) + b_ref[0]


def _modulation(cc, w_mod, b_mod):
    depth, d, n = w_mod.shape
    tn = 768
    return pl.pallas_call(
        _mod_kernel,
        grid=(depth, n // tn),
        in_specs=[
            pl.BlockSpec((8, d), lambda l, j: (0, 0)),
            pl.BlockSpec((1, d, tn), lambda l, j: (l, 0, j)),
            pl.BlockSpec((1, 1, tn), lambda l, j: (l, 0, j)),
        ],
        out_specs=pl.BlockSpec((1, 8, tn), lambda l, j: (l, 0, j)),
        out_shape=jax.ShapeDtypeStruct((depth, 8, n), F32),
        compiler_params=_params("parallel", "parallel"),
        name="modulation",
    )(cc, w_mod, b_mod.reshape(depth, 1, n))


def _inproj_kernel(x_ref, sh_ref, sc_ref, g_ref, w_ref, gq_ref, gk_ref, dgq_ref, dgk_ref,
                   g64_ref, g32_ref, cos_ref, sin_ref,
                   naq_ref, nak_ref, nav_ref, dqt_ref, dk_ref, dvt_ref, fu_ref, kabs_ref):
    x = x_ref[0]
    ms = jnp.mean(x * x, axis=-1, keepdims=True)
    h = x * lax.rsqrt(ms + EPS) * g_ref[...]
    h = h * (1.0 + sc_ref[0]) + sh_ref[0]
    hb = h.astype(BF16)

    def proj(c0, c1):
        return _dot(hb, w_ref[:, c0:c1])

    def group_norm(p, gmat_ref, gain):
        gms = _dot((p * p).astype(BF16), gmat_ref[...])
        return p * lax.rsqrt(gms + EPS) * gain

    def rope(y):
        lane = lax.broadcasted_iota(jnp.int32, y.shape, 1)
        nxt = pltpu.roll(y, DIFF_WIDTH - 8, axis=1)
        prv = pltpu.roll(y, 8, axis=1)
        partner = jnp.where((lane & 8) == 0, nxt, prv)
        return y * cos_ref[...] + partner * sin_ref[...]

    q = group_norm(proj(NA_Q0, NA_K0), g64_ref, gq_ref[...]) * (HEAD_DIM ** -0.5)
    naq_ref[0] = q.astype(BF16)
    k = group_norm(proj(NA_K0, NA_V0), g64_ref, gk_ref[...])
    nak_ref[0] = k.astype(BF16)
    nav_ref[0] = proj(NA_V0, DQ0).astype(BF16)
    dq = rope(group_norm(proj(DQ0, DK0), g32_ref, dgq_ref[...])) * (DIFF_QK_DIM ** -0.5 * LOG2E)
    dqt_ref[0] = dq.T.astype(BF16)
    dk = rope(group_norm(proj(DK0, DV0), g32_ref, dgk_ref[...]))
    dkb = dk.astype(BF16)
    dk_ref[0] = dkb
    kabs_ref[0, 0] = jnp.max(jnp.abs(dkb.astype(F32)), axis=0, keepdims=True)
    dvt_ref[0, 0] = proj(DV0, FN0).T.astype(BF16)
    fu_ref[0] = proj(FN0, IN_WIDTH)


def _in_projection(x, shift, scale, norm_g, w_in, gq, gk, dgq, dgk, cos_t, sin_t, tm):
    b, s, d = x.shape
    g64 = _group_mean_matrix(NA_WIDTH, HEAD_DIM)
    g32 = _group_mean_matrix(DIFF_WIDTH, DIFF_QK_DIM)
    const = lambda shape: pl.BlockSpec(shape, lambda bi, i: (0,) * len(shape))
    per_b = pl.BlockSpec((1, 1, d), lambda bi, i: (bi, 0, 0))
    tok = lambda w: pl.BlockSpec((1, tm, w), lambda bi, i: (bi, i, 0))
    tok_t = lambda w: pl.BlockSpec((1, w, tm), lambda bi, i: (bi, 0, i))
    sds = jax.ShapeDtypeStruct
    return pl.pallas_call(
        _inproj_kernel,
        grid=(b, s // tm),
        in_specs=[
            tok(d), per_b, per_b, const((1, d)), const((d, IN_WIDTH)),
            const((1, NA_WIDTH)), const((1, NA_WIDTH)), const((1, DIFF_WIDTH)), const((1, DIFF_WIDTH)),
            const((NA_WIDTH, NA_WIDTH)), const((DIFF_WIDTH, DIFF_WIDTH)),
            pl.BlockSpec((tm, DIFF_WIDTH), lambda bi, i: (i, 0)),
            pl.BlockSpec((tm, DIFF_WIDTH), lambda bi, i: (i, 0)),
        ],
        out_specs=[tok(NA_WIDTH), tok(NA_WIDTH), tok(NA_WIDTH),
                   tok_t(DIFF_WIDTH), tok(DIFF_WIDTH),
                   pl.BlockSpec((1, 1, DIFF_WIDTH, tm), lambda bi, i: (bi, i, 0, 0)), tok(FNET_WIDTH),
                   pl.BlockSpec((1, 1, 1, DIFF_WIDTH), lambda bi, i: (bi, i, 0, 0))],
        out_shape=[sds((b, s, NA_WIDTH), BF16), sds((b, s, NA_WIDTH), BF16), sds((b, s, NA_WIDTH), BF16),
                   sds((b, DIFF_WIDTH, s), BF16), sds((b, s, DIFF_WIDTH), BF16),
                   sds((b, s // tm, DIFF_WIDTH, tm), BF16), sds((b, s, FNET_WIDTH), F32),
                   sds((b, s // tm, 1, DIFF_WIDTH), F32)],
        compiler_params=_params("parallel", "parallel"),
        name="in_projection",
    )(x, shift, scale, norm_g, w_in, gq, gk, dgq, dgk, g64, g32, cos_t, sin_t)


def _rope_tables(s):
    pos = jnp.arange(s)
    rows = (pos // GRID_W).astype(F32)
    cols = (pos % GRID_W).astype(F32)
    m = DIFF_QK_DIM // 2
    inv = ROPE_BASE ** (-jnp.arange(0, m, 2, dtype=F32) / m)
    ang_r = rows[:, None] * inv[None, :]
    ang_c = cols[:, None] * inv[None, :]
    ang = jnp.concatenate([ang_r, ang_r, ang_c, ang_c], axis=1)
    sign = jnp.asarray(np.tile(np.repeat([-1.0, 1.0], 8), 2), F32)
    cos32 = jnp.cos(ang)
    sin32 = jnp.sin(ang) * sign[None, :]
    reps = DIFF_WIDTH // DIFF_QK_DIM
    return jnp.tile(cos32, (1, reps)), jnp.tile(sin32, (1, reps))


NA_GROUP_ROWS = 8
NA_HALF = NA_WIDTH // 2
NA_HEADS_PER_HALF = NA_HALF // HEAD_DIM


def _head_block_mask(rows_per_head, n_heads, width):
    r = lax.broadcasted_iota(jnp.int32, (rows_per_head * n_heads, width), 0) // rows_per_head
    c = lax.broadcasted_iota(jnp.int32, (rows_per_head * n_heads, width), 1) // (width // n_heads)
    return r == c


def _na_kernel(q_ref, kp_ref, k0_ref, kn_ref, vp_ref, v0_ref, vn_ref, kc_ref, vc_ref, bias_ref,
               o_ref, kbuf, vbuf, s_a, s_b):
    g = pl.program_id(1)
    blk = NA_GROUP_ROWS * GRID_W
    kbuf[0:blk] = kp_ref[0]
    kbuf[blk:2 * blk] = k0_ref[0]
    kbuf[2 * blk:3 * blk] = kn_ref[0]
    vbuf[0:blk] = vp_ref[0]
    vbuf[blk:2 * blk] = v0_ref[0]
    vbuf[2 * blk:3 * blk] = vn_ref[0]
    n_rows = pl.num_programs(1) * NA_GROUP_ROWS
    head_mask = _head_block_mask(GRID_W, NA_HEADS_PER_HALF, NA_HALF)
    n_keys = NA_KH * GRID_W

    s_bufs = (s_a, s_b)

    def window(i):
        r = g * NA_GROUP_ROWS + i
        row_start = jnp.clip(r - NA_KH // 2, 0, n_rows - NA_KH)
        off = pl.multiple_of((row_start - g * NA_GROUP_ROWS + NA_GROUP_ROWS) * GRID_W, GRID_W)
        return off, row_start - r + (NA_KH - 1)

    def scores(t, s_buf):
        i, hh = divmod(t, 2)
        off, d0 = window(i)
        cs = slice(hh * NA_HALF, (hh + 1) * NA_HALF)
        qh = q_ref[0, i * GRID_W:(i + 1) * GRID_W, cs]
        qst = jnp.concatenate([qh] * NA_HEADS_PER_HALF, axis=0)
        qst = jnp.where(head_mask, qst, jnp.zeros_like(qst))
        bias = jnp.concatenate([bias_ref[d0 + 2 * j, hh] for j in range(NA_KH // 2)], axis=1)
        s_buf[:, 0:n_keys] = _dot_nt(qst, kbuf[pl.ds(off, n_keys), cs]) + bias
        s_buf[:, n_keys:] = _dot_nt(qst, kc_ref[0, :, cs])

    def finish(t, s_buf):
        i, hh = divmod(t, 2)
        off, _ = window(i)
        cs = slice(hh * NA_HALF, (hh + 1) * NA_HALF)
        s = s_buf[...]
        m = jnp.max(s, axis=-1, keepdims=True)
        e = jnp.exp(s - m)
        l = jnp.sum(e, axis=-1, keepdims=True)
        eb = e.astype(BF16)
        o = _dot(eb[:, 0:n_keys], vbuf[pl.ds(off, n_keys), cs]) + _dot(eb[:, n_keys:], vc_ref[0, :, cs])
        o = jnp.where(head_mask, o * (1.0 / l), 0.0)
        out = o[0:GRID_W]
        for hl in range(1, NA_HEADS_PER_HALF):
            out = out + o[hl * GRID_W:(hl + 1) * GRID_W]
        o_ref[0, i * GRID_W:(i + 1) * GRID_W, cs] = out.astype(o_ref.dtype)

    n_jobs = 2 * NA_GROUP_ROWS
    scores(0, s_a)
    for t in range(n_jobs):
        if t + 1 < n_jobs:
            scores(t + 1, s_bufs[(t + 1) % 2])
        finish(t, s_bufs[t % 2])


def _na_bias_table(rpb):
    h, n_dr, _ = rpb.shape
    qc = np.arange(GRID_W)
    kc = np.arange(GRID_W)
    col_start = np.clip(qc - NA_KW // 2, 0, GRID_W - NA_KW)
    valid = (kc[None, :] >= col_start[:, None]) & (kc[None, :] < col_start[:, None] + NA_KW)
    pad = GRID_W - NA_KW
    period = 2 * GRID_W - 1
    rext = jnp.pad(rpb, ((0, 0), (0, 0), (pad, pad)))
    flat = jnp.tile(rext, (1, 1, GRID_W + 1))[:, :, :GRID_W * (period + 1)]
    hankel = flat.reshape(h, n_dr, GRID_W, period + 1)[..., :GRID_W]
    toep = hankel[:, :, ::-1, :]
    toep = jnp.where(jnp.asarray(valid)[None, None], toep, NEG_BIG)
    pair = jnp.concatenate([toep[:, :-1], toep[:, 1:]], axis=-1)
    pair = pair.reshape(2, NA_HEADS_PER_HALF, n_dr - 1, GRID_W, 2 * GRID_W).transpose(2, 0, 1, 3, 4)
    return pair.reshape(n_dr - 1, 2, NA_HEADS_PER_HALF * GRID_W, 2 * GRID_W).astype(F32)


def _neighbourhood_attention(q, k, v, kc, vc, rpb):
    b, s, _ = q.shape
    n_groups = s // (NA_GROUP_ROWS * GRID_W)
    blk = NA_GROUP_ROWS * GRID_W
    c = kc.shape[1]
    bias = _na_bias_table(rpb)
    cur = pl.BlockSpec((1, blk, NA_WIDTH), lambda bi, g: (bi, g, 0))
    prv = pl.BlockSpec((1, blk, NA_WIDTH), lambda bi, g: (bi, jnp.maximum(g - 1, 0), 0))
    nxt = pl.BlockSpec((1, blk, NA_WIDTH), lambda bi, g: (bi, jnp.minimum(g + 1, n_groups - 1), 0))
    ctx = pl.BlockSpec((1, c, NA_WIDTH), lambda bi, g: (bi, 0, 0))
    return pl.pallas_call(
        _na_kernel,
        grid=(b, n_groups),
        in_specs=[cur, prv, cur, nxt, prv, cur, nxt, ctx, ctx,
                  pl.BlockSpec(bias.shape, lambda bi, g: (0, 0, 0, 0))],
        out_specs=cur,
        out_shape=jax.ShapeDtypeStruct((b, s, NA_WIDTH), BF16),
        scratch_shapes=[pltpu.VMEM((3 * blk, NA_WIDTH), BF16), pltpu.VMEM((3 * blk, NA_WIDTH), BF16)]
                       + [pltpu.VMEM((NA_HEADS_PER_HALF * GRID_W, NA_KH * GRID_W + c), F32)] * 2,
        compiler_params=_params("parallel", "arbitrary"),
        name="neighbourhood_attention",
    )(q, k, k, k, v, v, v, kc, vc, bias)


def _dense_na_kernel(q_ref, k_ref, v_ref, o_ref):
    c = q_ref.shape[1]
    head_mask = _head_block_mask(c, NA_HEADS_PER_HALF, NA_HALF)
    for hh in range(2):
        cs = slice(hh * NA_HALF, (hh + 1) * NA_HALF)
        qh = q_ref[0, :, cs]
        qst = jnp.concatenate([qh] * NA_HEADS_PER_HALF, axis=0)
        qst = jnp.where(head_mask, qst, jnp.zeros_like(qst))
        s = _dot_nt(qst, k_ref[0, :, cs])
        m = jnp.max(s, axis=-1, keepdims=True)
        e = jnp.exp(s - m)
        l = jnp.sum(e, axis=-1, keepdims=True)
        o = _dot(e.astype(BF16), v_ref[0, :, cs])
        o = jnp.where(head_mask, o * (1.0 / l), 0.0)
        out = o[0:c]
        for hl in range(1, NA_HEADS_PER_HALF):
            out = out + o[hl * c:(hl + 1) * c]
        o_ref[0, :, cs] = out.astype(o_ref.dtype)


def _dense_na_attention(q, k, v):
    b, c, w = q.shape
    spec = pl.BlockSpec((1, c, w), lambda bi: (bi, 0, 0))
    return pl.pallas_call(
        _dense_na_kernel,
        grid=(b,),
        in_specs=[spec, spec, spec],
        out_specs=spec,
        out_shape=jax.ShapeDtypeStruct((b, c, w), BF16),
        compiler_params=_params("parallel"),
        name="context_dense_attention",
    )(q, k, v)


N_DIFF_STREAMS = 2 * DIFF_HEADS


DIFF_HEAD_PAIRS = DIFF_HEADS // 2
DIFF_MIN_TRUSTED_SUM = 2.0 ** -80


def _diff_kernel(*refs, lam_init, has_ctx):
    n_in = 7 if has_ctx else 5
    lam_ref, qt_ref, k_ref, vt_ref, kmax_ref = refs[:5]
    kc_ref, vct_ref = refs[5:7] if has_ctx else (None, None)
    o_ref, qs, s_a, s_b, m_scr, l_scr, acc = refs[n_in:]
    s_bufs = (s_a, s_b)
    tq = qt_ref.shape[2]
    n_chunks, _, tk = vt_ref.shape[1:]
    pair_w = 4 * tq

    qt = qt_ref[0]
    stream = lax.broadcasted_iota(jnp.int32, qt.shape, 0) // DIFF_QK_DIM
    for st in range(N_DIFF_STREAMS):
        qs[:, st * tq:(st + 1) * tq] = jnp.where(stream == st, qt, jnp.zeros_like(qt))
    bound = jnp.sum(jnp.abs(qs[...].astype(F32)) * kmax_ref[0], axis=0, keepdims=True)
    m_scr[...] = bound * (1.0 + 2.0 ** -10) + 2.0 ** -10
    l_scr[...] = jnp.zeros(l_scr.shape, F32)
    acc[...] = jnp.zeros(acc.shape, F32)

    def scores(kk, s_buf):
        n = kk.shape[0]
        for pair in range(DIFF_HEAD_PAIRS):
            cols = slice(pair * pair_w, (pair + 1) * pair_w)
            s_buf[0:n, cols] = _dot(kk, qs[:, cols])

    def absorb_bounded(s_buf, n, vt_rows):
        for pair in range(DIFF_HEAD_PAIRS):
            cols = slice(pair * pair_w, (pair + 1) * pair_w)
            p = jnp.exp2(s_buf[0:n, cols] - m_scr[:, cols])
            l_scr[:, cols] += jnp.sum(p, axis=0, keepdims=True)
            pb = p.astype(BF16)
            for hh in range(2):
                h = 2 * pair + hh
                rows = slice(h * DIFF_V_DIM, (h + 1) * DIFF_V_DIM)
                hc = slice(hh * 2 * tq, (hh + 1) * 2 * tq)
                acc[rows, :] += _dot(vt_rows(rows), pb[:, hc])

    def absorb(s_buf, n, vt_rows):
        for pair in range(DIFF_HEAD_PAIRS):
            cols = slice(pair * pair_w, (pair + 1) * pair_w)
            s = s_buf[0:n, cols]
            m_old = m_scr[:, cols]
            m_new = jnp.maximum(m_old, jnp.max(s, axis=0, keepdims=True))
            alpha = jnp.exp2(m_old - m_new)
            p = jnp.exp2(s - m_new)
            l_scr[:, cols] = alpha * l_scr[:, cols] + jnp.sum(p, axis=0, keepdims=True)
            m_scr[:, cols] = m_new
            pb = p.astype(BF16)
            for hh in range(2):
                h = 2 * pair + hh
                rows = slice(h * DIFF_V_DIM, (h + 1) * DIFF_V_DIM)
                hc = slice(hh * 2 * tq, (hh + 1) * 2 * tq)
                acc[rows, :] = acc[rows, :] * alpha[:, hc] + _dot(vt_rows(rows), pb[:, hc])

    n_items = n_chunks + (1 if has_ctx else 0)

    def scores_item(t, buf):
        if has_ctx and isinstance(t, int) and t == n_chunks:
            scores(kc_ref[0], buf)
        else:
            start = t * tk if isinstance(t, int) else pl.multiple_of(t * tk, tk)
            scores(k_ref[0, pl.ds(start, tk), :], buf)

    def all_items(absorb_fn):
        def absorb_item(t, buf):
            if has_ctx and isinstance(t, int) and t == n_chunks:
                absorb_fn(buf, kc_ref.shape[1], lambda rows: vct_ref[0, rows, :])
            else:
                absorb_fn(buf, tk, lambda rows: vt_ref[0, t, rows, :])

        scores_item(0, s_a)
        n_pairs = (n_chunks - 1) // 2

        def body(i, carry):
            t = 2 * i
            scores_item(t + 1, s_b)
            absorb_item(t, s_a)
            scores_item(t + 2, s_a)
            absorb_item(t + 1, s_b)
            return carry

        lax.fori_loop(0, n_pairs, body, 0)
        for t in range(2 * n_pairs, n_items):
            if t + 1 < n_items:
                scores_item(t + 1, s_bufs[(t + 1) % 2])
            absorb_item(t, s_bufs[t % 2])

    all_items(absorb_bounded)

    @pl.when(jnp.min(l_scr[...]) < DIFF_MIN_TRUSTED_SUM)
    def _():
        m_scr[...] = jnp.full(m_scr.shape, NEG_BIG, F32)
        l_scr[...] = jnp.zeros(l_scr.shape, F32)
        acc[...] = jnp.zeros(acc.shape, F32)
        all_items(absorb)

    lp = lam_ref[...]
    lam = (jnp.exp(jnp.sum(lp[0:1] * lp[1:2], axis=1, keepdims=True))
           - jnp.exp(jnp.sum(lp[2:3] * lp[3:4], axis=1, keepdims=True)) + lam_init)
    inv_l = 1.0 / l_scr[...]
    outs = []
    for h in range(DIFF_HEADS):
        rows = slice(h * DIFF_V_DIM, (h + 1) * DIFF_V_DIM)
        o1 = acc[rows, 0:tq] * inv_l[:, 2 * h * tq:(2 * h + 1) * tq]
        o2 = acc[rows, tq:2 * tq] * inv_l[:, (2 * h + 1) * tq:(2 * h + 2) * tq]
        outs.append(o1 - lam * o2)
    o_ref[0] = jnp.concatenate(outs, axis=0).T


def _diff_attention(lam_p, qt, k, vt, kmax, lam_init, tq, ctx_kv=None):
    b, w, sq = qt.shape
    n_chunks, _, tk = vt.shape[1:]
    whole = lambda a: pl.BlockSpec((1,) + a.shape[1:], lambda bi, i: (bi,) + (0,) * (a.ndim - 1))
    operands = [lam_p, qt, k, vt, kmax]
    in_specs = [
        pl.BlockSpec(lam_p.shape, lambda bi, i: (0, 0)),
        pl.BlockSpec((1, w, tq), lambda bi, i: (bi, 0, i)),
        whole(k), whole(vt), whole(kmax),
    ]
    if ctx_kv is not None:
        operands += list(ctx_kv)
        in_specs += [whole(ctx_kv[0]), whole(ctx_kv[1])]
    return pl.pallas_call(
        functools.partial(_diff_kernel, lam_init=lam_init, has_ctx=ctx_kv is not None),
        grid=(b, sq // tq),
        in_specs=in_specs,
        out_specs=pl.BlockSpec((1, tq, w), lambda bi, i: (bi, i, 0)),
        out_shape=jax.ShapeDtypeStruct((b, sq, w), F32),
        scratch_shapes=[
            pltpu.VMEM((w, N_DIFF_STREAMS * tq), BF16),
            pltpu.VMEM((tk, N_DIFF_STREAMS * tq), F32),
            pltpu.VMEM((tk, N_DIFF_STREAMS * tq), F32),
            pltpu.VMEM((1, N_DIFF_STREAMS * tq), F32),
            pltpu.VMEM((1, N_DIFF_STREAMS * tq), F32),
            pltpu.VMEM((w, 2 * tq), F32),
        ],
        compiler_params=_params("parallel", "arbitrary"),
        name="differential_attention",
    )(*operands)


FFT_N1 = 64
FFT_N1_STEP = 8


def _dft_tables(n):
    idx = np.arange(n)
    ang = 2.0 * np.pi * ((idx[:, None] * idx[None, :]) % n) / n
    return np.cos(ang), np.sin(ang)


def _fft_stage_a_kernel(x_ref, t_ref, twc_ref, tws_ref, ur_ref, ui_ref):
    n2 = x_ref.shape[1]
    t = t_ref[...].astype(BF16)
    for i in range(FFT_N1_STEP):
        xs = x_ref[0, :, i * FNET_WIDTH:(i + 1) * FNET_WIDTH].astype(BF16)
        u = _dot(t, xs)
        ur, ui = u[0:n2], u[n2:2 * n2]
        c = jnp.concatenate([twc_ref[i]] * (FNET_WIDTH // LANES), axis=1)
        s = jnp.concatenate([tws_ref[i]] * (FNET_WIDTH // LANES), axis=1)
        ur_ref[0, i] = ur * c + ui * s
        ui_ref[0, i] = ui * c - ur * s


def _fft_stage_c_kernel(ur_ref, ui_ref, c_ref, s_ref, pr_ref, pi_ref):
    ur = ur_ref[0].astype(BF16)
    ui = ui_ref[0].astype(BF16)
    c1 = c_ref[...].astype(BF16)
    s1 = s_ref[...].astype(BF16)
    pr_ref[0] = _dot(c1, ur) + _dot(s1, ui)
    pi_ref[0] = _dot(c1, ui) - _dot(s1, ur)


def _position_dft(fu):
    b, l, w = fu.shape
    n1, n2 = FFT_N1, l // FFT_N1
    scale = 1.0 / math.sqrt(l * FNET_GROUP_DIM)
    c2, s2 = _dft_tables(n2)
    t2 = jnp.asarray(np.concatenate([c2, -s2], axis=0) * scale, dtype=F32)
    ang = (2.0 * jnp.pi / l) * (jnp.arange(n1, dtype=F32)[:, None] * jnp.arange(n2, dtype=F32)[None, :])
    twc = jnp.broadcast_to(jnp.cos(ang)[:, :, None], (n1, n2, LANES))
    tws = jnp.broadcast_to(jnp.sin(ang)[:, :, None], (n1, n2, LANES))
    xv = fu.reshape(b, n2, n1 * w)
    sds = jax.ShapeDtypeStruct
    ur, ui = pl.pallas_call(
        _fft_stage_a_kernel,
        grid=(b, n1 // FFT_N1_STEP),
        in_specs=[
            pl.BlockSpec((1, n2, FFT_N1_STEP * w), lambda bi, j: (bi, 0, j)),
            pl.BlockSpec((2 * n2, n2), lambda bi, j: (0, 0)),
            pl.BlockSpec((FFT_N1_STEP, n2, LANES), lambda bi, j: (j, 0, 0)),
            pl.BlockSpec((FFT_N1_STEP, n2, LANES), lambda bi, j: (j, 0, 0)),
        ],
        out_specs=[pl.BlockSpec((1, FFT_N1_STEP, n2, w), lambda bi, j: (bi, j, 0, 0))] * 2,
        out_shape=[sds((b, n1, n2, w), F32)] * 2,
        compiler_params=_params("parallel", "parallel"),
        name="position_dft_stage_a",
    )(xv, t2, twc, tws)
    c1, s1 = _dft_tables(n1)
    tn = 4096
    cols = n2 * w
    blk = pl.BlockSpec((1, n1, tn), lambda bi, j: (bi, 0, j))
    tab = pl.BlockSpec((n1, n1), lambda bi, j: (0, 0))
    pr, pi = pl.pallas_call(
        _fft_stage_c_kernel,
        grid=(b, cols // tn),
        in_specs=[blk, blk, tab, tab],
        out_specs=[blk, blk],
        out_shape=[sds((b, n1, cols), F32)] * 2,
        compiler_params=_params("parallel", "parallel"),
        name="position_dft_stage_c",
    )(ur.reshape(b, n1, cols), ui.reshape(b, n1, cols), jnp.asarray(c1, F32), jnp.asarray(s1, F32))
    return pr.reshape(b, l, w), pi.reshape(b, l, w)


def _direct_dft_kernel(x_ref, t_ref, pr_ref, pi_ref):
    l = x_ref.shape[1]
    u = _dot(t_ref[...].astype(BF16), x_ref[0].astype(BF16))
    pr_ref[0] = u[0:l]
    pi_ref[0] = u[l:2 * l]


def _position_dft_direct(fu):
    b, l, w = fu.shape
    scale = 1.0 / math.sqrt(l * FNET_GROUP_DIM)
    c, s = _dft_tables(l)
    t = jnp.asarray(np.concatenate([c, -s], axis=0) * scale, dtype=F32)
    spec = pl.BlockSpec((1, l, w), lambda bi: (bi, 0, 0))
    return pl.pallas_call(
        _direct_dft_kernel,
        grid=(b,),
        in_specs=[spec, pl.BlockSpec((2 * l, l), lambda bi: (0, 0))],
        out_specs=[spec, spec],
        out_shape=[jax.ShapeDtypeStruct((b, l, w), F32)] * 2,
        compiler_params=_params("parallel"),
        name="context_position_dft",
    )(fu, t)


def _merge_kernel(na_ref, df_ref, pr_ref, pi_ref, x_ref, g1_ref, sh2_ref, sc2_ref, n2g_ref, subg_ref,
                  g64_ref, wc_ref, ws_ref, wf_ref, wo_ref, x1_ref, h2_ref, *, diff_scale):
    d = df_ref[0]
    gms = _dot((d * d).astype(BF16), g64_ref[...])
    dn = d * lax.rsqrt(gms + EPS) * subg_ref[...] * diff_scale
    fr = (_dot(pr_ref[0].astype(BF16), wc_ref[...].astype(BF16))
          + _dot(pi_ref[0].astype(BF16), ws_ref[...].astype(BF16)))
    four = _dot(fr.astype(BF16), wf_ref[...])
    y = (_dot(na_ref[0], wo_ref[0:NA_WIDTH, :])
         + _dot(dn.astype(BF16), wo_ref[NA_WIDTH:NA_WIDTH + DIFF_WIDTH, :])
         + _dot(four.astype(BF16), wo_ref[NA_WIDTH + DIFF_WIDTH:, :]))
    x1 = x_ref[0] + g1_ref[0] * y
    x1_ref[0] = x1
    ms = jnp.mean(x1 * x1, axis=-1, keepdims=True)
    h = x1 * lax.rsqrt(ms + EPS) * n2g_ref[...]
    h2_ref[0] = (h * (1.0 + sc2_ref[0]) + sh2_ref[0]).astype(BF16)


def _channel_dft_tables():
    c, s = _dft_tables(FNET_GROUP_DIM)
    eye = np.eye(FNET_GROUPS)
    return jnp.asarray(np.kron(eye, c), F32), jnp.asarray(np.kron(eye, s), F32)


def _merge(na_o, diff_o, pr, pi, x, g1, sh2, sc2, norm2_g, subln_g, w_four, w_out, lam_init, tm):
    b, s, d = x.shape
    wc, ws = _channel_dft_tables()
    g64 = _group_mean_matrix(DIFF_WIDTH, DIFF_V_DIM)
    const = lambda shape: pl.BlockSpec(shape, lambda bi, i: (0,) * len(shape))
    per_b = pl.BlockSpec((1, 1, d), lambda bi, i: (bi, 0, 0))
    tok = lambda w: pl.BlockSpec((1, tm, w), lambda bi, i: (bi, i, 0))
    sq = (FNET_WIDTH, FNET_WIDTH)
    return pl.pallas_call(
        functools.partial(_merge_kernel, diff_scale=1.0 - lam_init),
        grid=(b, s // tm),
        in_specs=[tok(NA_WIDTH), tok(DIFF_WIDTH), tok(FNET_WIDTH), tok(FNET_WIDTH), tok(d),
                  per_b, per_b, per_b, const((1, d)), const((1, DIFF_WIDTH)),
                  const((DIFF_WIDTH, DIFF_WIDTH)), const(sq), const(sq), const(sq), const((d, d))],
        out_specs=[tok(d), tok(d)],
        out_shape=[jax.ShapeDtypeStruct((b, s, d), F32), jax.ShapeDtypeStruct((b, s, d), BF16)],
        compiler_params=_params("parallel", "parallel"),
        name="merge_projection",
    )(na_o, diff_o, pr, pi, x, g1, sh2, sc2, norm2_g, subln_g, g64, wc, ws, w_four, w_out)


FFN_TN = 256
FFN_HALO = BF16_SUBLANES


def _ffn_kernel(hp_ref, h_ref, hn_ref, wu_ref, cw_ref, cb_ref, wd_ref, x1_ref, g2_ref, o_ref,
                hcat, ug_a, uv_a, ug_b, uv_b, a_a, a_b, acc):
    i = pl.program_id(1)
    tm = h_ref.shape[1]
    nj = wd_ref.shape[0]
    hp = hp_ref[0]
    hn = hn_ref[0]
    hcat[0:FFN_HALO] = jnp.where(i > 0, hp, jnp.zeros_like(hp))
    hcat[FFN_HALO:FFN_HALO + tm] = h_ref[0]
    hcat[FFN_HALO + tm:] = jnp.where(i < pl.num_programs(1) - 1, hn, jnp.zeros_like(hn))
    acc[...] = jnp.zeros(acc.shape, F32)
    bufs = ((ug_a, uv_a), (ug_b, uv_b))
    a_bufs = (a_a, a_b)

    def project_up(j, buf):
        buf[0][...] = _dot(hcat[...], wu_ref[j])
        buf[1][...] = _dot(hcat[...], wu_ref[nj + j])

    def conv(u, cw, cb):
        total = u.shape[0]
        prv = pltpu.roll(u, 1, axis=0)[FFN_HALO:FFN_HALO + tm]
        cur = u[FFN_HALO:FFN_HALO + tm]
        nxt = pltpu.roll(u, total - 1, axis=0)[FFN_HALO:FFN_HALO + tm]
        return prv * cw[0:1] + cur * cw[1:2] + nxt * cw[2:3] + cb

    def gate(j, buf, a_buf):
        g = conv(buf[0][...], cw_ref[j], cb_ref[j])
        val = conv(buf[1][...], cw_ref[nj + j], cb_ref[nj + j])
        a_buf[...] = (g * _sigmoid(g) * val).astype(BF16)

    def step(t, par, do_up, do_gate, do_down):
        if do_up:
            project_up(t + 1, bufs[1 - par])
        if do_gate:
            gate(t, bufs[par], a_bufs[par])
        if do_down:
            acc[...] += _dot(a_bufs[1 - par][...], wd_ref[t - 1])

    project_up(0, bufs[0])
    for t in range(nj + 1):
        step(t, t % 2, t + 1 < nj, t < nj, t > 0)
    o_ref[0] = x1_ref[0] + g2_ref[0] * acc[...]


def _conv_ffn(h2, x1, g2, w_up, conv_w, conv_b, w_down, tm):
    b, s, d = x1.shape
    nj = w_down.shape[0]
    per_tile = tm // FFN_HALO
    n_halo_blocks = s // FFN_HALO
    rows = tm + 2 * FFN_HALO
    resident = lambda a: pl.BlockSpec(a.shape, lambda bi, i: (0,) * a.ndim, pipeline_mode=pl.Buffered(1))
    return pl.pallas_call(
        _ffn_kernel,
        grid=(b, s // tm),
        in_specs=[
            pl.BlockSpec((1, FFN_HALO, d), lambda bi, i: (bi, jnp.maximum(i * per_tile - 1, 0), 0)),
            pl.BlockSpec((1, tm, d), lambda bi, i: (bi, i, 0)),
            pl.BlockSpec((1, FFN_HALO, d), lambda bi, i: (bi, jnp.minimum((i + 1) * per_tile, n_halo_blocks - 1), 0)),
            resident(w_up), resident(conv_w), resident(conv_b), resident(w_down),
            pl.BlockSpec((1, tm, d), lambda bi, i: (bi, i, 0)),
            pl.BlockSpec((1, 1, d), lambda bi, i: (bi, 0, 0)),
        ],
        out_specs=pl.BlockSpec((1, tm, d), lambda bi, i: (bi, i, 0)),
        out_shape=jax.ShapeDtypeStruct((b, s, d), F32),
        scratch_shapes=[pltpu.VMEM((rows, d), BF16)] + [pltpu.VMEM((rows, FFN_TN), F32)] * 4
                       + [pltpu.VMEM((tm, FFN_TN), BF16)] * 2 + [pltpu.VMEM((tm, d), F32)],
        compiler_params=_params("parallel", "arbitrary"),
        name="conv_ffn",
    )(h2, h2, h2, w_up, conv_w, conv_b, w_down, x1, g2)


def kernel(x, c, ctx, c_ctx, w_mod, b_mod, norm1_g, w_in, na_q_g, na_k_g, na_rpb, diff_q_g, diff_k_g,
           diff_lambda, diff_subln_g, w_fourier, w_out, norm2_g, w_up, conv_w, conv_b, w_down):
    b, s, d = x.shape
    n_ctx = ctx.shape[1]
    depth = w_mod.shape[0]

    cc = jnp.zeros((8, d), F32).at[0:b].set(c).at[b].set(c_ctx)
    mod = _modulation(cc, w_mod, b_mod)

    cos_x, sin_x = _rope_tables(s)
    cos_c = jnp.ones((n_ctx, DIFF_WIDTH), F32)
    sin_c = jnp.zeros((n_ctx, DIFF_WIDTH), F32)

    cx = ctx
    for l in range(depth):
        lam_init = 0.8 - 0.6 * math.exp(-0.3 * l)
        mx = mod[l, 0:b].reshape(b, 1, 6, d)
        mc = jnp.broadcast_to(mod[l, b].reshape(1, 1, 6, d), (b, 1, 6, d))
        sh1, sc1, g1, sh2, sc2, g2 = (mx[:, :, t] for t in range(6))
        csh1, csc1, cg1, csh2, csc2, cg2 = (mc[:, :, t] for t in range(6))

        w_in_b = w_in[l].astype(BF16)
        w_out_b = w_out[l].astype(BF16)
        w_four_b = w_fourier[l].astype(BF16)
        n_up = w_up.shape[2] // FFN_TN
        w_up_b = w_up[l].astype(BF16).reshape(d, n_up, FFN_TN).transpose(1, 0, 2)
        w_down_b = w_down[l].astype(BF16).reshape(n_up // 2, FFN_TN, d)
        cw = conv_w[l].reshape(3, n_up, FFN_TN).transpose(1, 0, 2)
        n1g = norm1_g[l].reshape(1, d)
        n2g = norm2_g[l].reshape(1, d)
        gq = jnp.tile(na_q_g[l], NA_HEADS).reshape(1, NA_WIDTH)
        gk = jnp.tile(na_k_g[l], NA_HEADS).reshape(1, NA_WIDTH)
        dgq = jnp.tile(diff_q_g[l], 2 * DIFF_HEADS).reshape(1, DIFF_WIDTH)
        dgk = jnp.tile(diff_k_g[l], 2 * DIFF_HEADS).reshape(1, DIFF_WIDTH)
        subg = jnp.tile(diff_subln_g[l], DIFF_HEADS).reshape(1, DIFF_WIDTH)
        cb = conv_b[l].reshape(n_up, 1, FFN_TN)

        naq, nak, nav, dqt, dk, dvt, fu, kabs = _in_projection(
            x, sh1, sc1, n1g, w_in_b, gq, gk, dgq, dgk, cos_x, sin_x, tm=512)
        cnaq, cnak, cnav, cdqt, cdk, cdvt, cfu, ckabs = _in_projection(
            cx, csh1, csc1, n1g, w_in_b, gq, gk, dgq, dgk, cos_c, sin_c, tm=n_ctx)
        kmax_c = jnp.max(ckabs, axis=1).reshape(b, DIFF_WIDTH, 1)
        kmax_x = jnp.maximum(jnp.max(kabs, axis=1).reshape(b, DIFF_WIDTH, 1), kmax_c)

        na_x = _neighbourhood_attention(naq, nak, nav, cnak, cnav, na_rpb[l])
        diff_x = _diff_attention(diff_lambda[l], dqt, dk, dvt, kmax_x, lam_init, tq=128,
                                 ctx_kv=(cdk, cdvt.reshape(b, DIFF_WIDTH, n_ctx)))
        pr, pi = _position_dft(fu)
        x1, h2 = _merge(na_x, diff_x, pr, pi, x, g1, sh2, sc2, n2g, subg, w_four_b, w_out_b, lam_init, tm=512)
        x = _conv_ffn(h2, x1, g2, w_up_b, cw, cb, w_down_b, tm=512)

        if l < depth - 1:
            na_c = _dense_na_attention(cnaq, cnak, cnav)
            diff_c = _diff_attention(diff_lambda[l], cdqt, cdk, cdvt, kmax_c, lam_init, tq=128)
            cpr, cpi = _position_dft_direct(cfu)
            cx1, ch2 = _merge(na_c, diff_c, cpr, cpi, cx, cg1, csh2, csc2, n2g, subg, w_four_b, w_out_b,
                              lam_init, tm=n_ctx)
            cx = _conv_ffn(ch2, cx1, cg2, w_up_b, cw, cb, w_down_b, tm=n_ctx)
    return x
```

```python
import functools
import math

import numpy as np
import jax
import jax.numpy as jnp
from jax import lax
from jax.experimental import pallas as pl
from jax.experimental.pallas import tpu as pltpu

F32 = jnp.float32
BF16 = jnp.bfloat16

GRID_W = 64
EPS = 1e-6
ROPE_BASE = 10000.0
HEAD_DIM = 64
NA_HEADS = 8
NA_WIDTH = NA_HEADS * HEAD_DIM
NA_KH = 8
NA_KW = 16
DIFF_HEADS = 4
DIFF_QK_DIM = 32
DIFF_V_DIM = 64
DIFF_WIDTH = DIFF_HEADS * DIFF_V_DIM
FNET_GROUPS = 4
FNET_GROUP_DIM = 64
FNET_WIDTH = FNET_GROUPS * FNET_GROUP_DIM
NA_Q0 = 0
NA_K0 = NA_Q0 + NA_WIDTH
NA_V0 = NA_K0 + NA_WIDTH
DQ0 = NA_V0 + NA_WIDTH
DK0 = DQ0 + DIFF_WIDTH
DV0 = DK0 + DIFF_WIDTH
FN0 = DV0 + DIFF_WIDTH
IN_WIDTH = FN0 + FNET_WIDTH
NEG_BIG = -1e30
LOG2E = 1.4426950408889634

VMEM_LIMIT_BYTES = 52 * 1024 * 1024
LANES = 128
BF16_SUBLANES = 16


def _params(*sem, flags=None):
    return pltpu.CompilerParams(dimension_semantics=sem, vmem_limit_bytes=VMEM_LIMIT_BYTES, flags=flags)


def _dot(a, b):
    return jnp.dot(a, b, preferred_element_type=F32)


def _dot_nt(a, b):
    return lax.dot_general(a, b, (((1,), (1,)), ((), ())), preferred_element_type=F32)


def _sigmoid(x):
    return 1.0 / (1.0 + jnp.exp(-x))


def _group_mean_matrix(width, group):
    idx = np.arange(width) // group
    return jnp.asarray((idx[:, None] == idx[None, :]).astype(np.float32) / group, dtype=BF16)


def _mod_kernel(c_ref, w_ref, b_ref, o_ref):
    c = c_ref[...]
    s = c * _sigmoid(c)
    s_hi = s.astype(BF16)
    s_lo = (s - s_hi.astype(F32)).astype(BF16)
    w = w_ref[0]
    w_hi = w.astype(BF16)
    w_lo = (w - w_hi.astype(F32)).astype(BF16)
    o_ref[0] = _dot(s_hi, w_hi) + _dot(s_hi, w_lo) + _dot(s_lo, w_hi) + b_ref[0]


def _modulation(cc, w_mod, b_mod):
    depth, d, n = w_mod.shape
    tn = 768
    return pl.pallas_call(
        _mod_kernel,
        grid=(depth, n // tn),
        in_specs=[
            pl.BlockSpec((8, d), lambda l, j: (0, 0)),
            pl.BlockSpec((1, d, tn), lambda l, j: (l, 0, j)),
            pl.BlockSpec((1, 1, tn), lambda l, j: (l, 0, j)),
        ],
        out_specs=pl.BlockSpec((1, 8, tn), lambda l, j: (l, 0, j)),
        out_shape=jax.ShapeDtypeStruct((depth, 8, n), F32),
        compiler_params=_params("parallel", "parallel"),
        name="modulation",
    )(cc, w_mod, b_mod.reshape(depth, 1, n))


def _inproj_kernel(x_ref, sh_ref, sc_ref, g_ref, w_ref, gq_ref, gk_ref, dgq_ref, dgk_ref,
                   g64_ref, g32_ref, cos_ref, sin_ref,
                   naq_ref, nak_ref, nav_ref, dqt_ref, dk_ref, dvt_ref, fu_ref, kabs_ref):
    x = x_ref[0]
    ms = jnp.mean(x * x, axis=-1, keepdims=True)
    h = x * lax.rsqrt(ms + EPS) * g_ref[...]
    h = h * (1.0 + sc_ref[0]) + sh_ref[0]
    hb = h.astype(BF16)

    def proj(c0, c1):
        return _dot(hb, w_ref[:, c0:c1])

    def group_norm(p, gmat_ref, gain):
        gms = _dot((p * p).astype(BF16), gmat_ref[...])
        return p * lax.rsqrt(gms + EPS) * gain

    def rope(y):
        lane = lax.broadcasted_iota(jnp.int32, y.shape, 1)
        nxt = pltpu.roll(y, DIFF_WIDTH - 8, axis=1)
        prv = pltpu.roll(y, 8, axis=1)
        partner = jnp.where((lane & 8) == 0, nxt, prv)
        return y * cos_ref[...] + partner * sin_ref[...]

    q = group_norm(proj(NA_Q0, NA_K0), g64_ref, gq_ref[...]) * (HEAD_DIM ** -0.5)
    naq_ref[0] = q.astype(BF16)
    k = group_norm(proj(NA_K0, NA_V0), g64_ref, gk_ref[...])
    nak_ref[0] = k.astype(BF16)
    nav_ref[0] = proj(NA_V0, DQ0).astype(BF16)
    dq = rope(group_norm(proj(DQ0, DK0), g32_ref, dgq_ref[...])) * (DIFF_QK_DIM ** -0.5 * LOG2E)
    dqt_ref[0] = dq.T.astype(BF16)
    dk = rope(group_norm(proj(DK0, DV0), g32_ref, dgk_ref[...]))
    dkb = dk.astype(BF16)
    dk_ref[0] = dkb
    kabs_ref[0, 0] = jnp.max(jnp.abs(dkb.astype(F32)), axis=0, keepdims=True)
    dvt_ref[0, 0] = proj(DV0, FN0).T.astype(BF16)
    fu_ref[0] = proj(FN0, IN_WIDTH).astype(BF16)


def _in_projection(x, shift, scale, norm_g, w_in, gq, gk, dgq, dgk, cos_t, sin_t, tm):
    b, s, d = x.shape
    g64 = _group_mean_matrix(NA_WIDTH, HEAD_DIM)
    g32 = _group_mean_matrix(DIFF_WIDTH, DIFF_QK_DIM)
    const = lambda shape: pl.BlockSpec(shape, lambda bi, i: (0,) * len(shape))
    per_b = pl.BlockSpec((1, 1, d), lambda bi, i: (bi, 0, 0))
    tok = lambda w: pl.BlockSpec((1, tm, w), lambda bi, i: (bi, i, 0))
    tok_t = lambda w: pl.BlockSpec((1, w, tm), lambda bi, i: (bi, 0, i))
    sds = jax.ShapeDtypeStruct
    return pl.pallas_call(
        _inproj_kernel,
        grid=(b, s // tm),
        in_specs=[
            tok(d), per_b, per_b, const((1, d)), const((d, IN_WIDTH)),
            const((1, NA_WIDTH)), const((1, NA_WIDTH)), const((1, DIFF_WIDTH)), const((1, DIFF_WIDTH)),
            const((NA_WIDTH, NA_WIDTH)), const((DIFF_WIDTH, DIFF_WIDTH)),
            pl.BlockSpec((tm, DIFF_WIDTH), lambda bi, i: (i, 0)),
            pl.BlockSpec((tm, DIFF_WIDTH), lambda bi, i: (i, 0)),
        ],
        out_specs=[tok(NA_WIDTH), tok(NA_WIDTH), tok(NA_WIDTH),
                   tok_t(DIFF_WIDTH), tok(DIFF_WIDTH),
                   pl.BlockSpec((1, 1, DIFF_WIDTH, tm), lambda bi, i: (bi, i, 0, 0)), tok(FNET_WIDTH),
                   pl.BlockSpec((1, 1, 1, DIFF_WIDTH), lambda bi, i: (bi, i, 0, 0))],
        out_shape=[sds((b, s, NA_WIDTH), BF16), sds((b, s, NA_WIDTH), BF16), sds((b, s, NA_WIDTH), BF16),
                   sds((b, DIFF_WIDTH, s), BF16), sds((b, s, DIFF_WIDTH), BF16),
                   sds((b, s // tm, DIFF_WIDTH, tm), BF16), sds((b, s, FNET_WIDTH), BF16),
                   sds((b, s // tm, 1, DIFF_WIDTH), F32)],
        compiler_params=_params("parallel", "parallel"),
        name="in_projection",
    )(x, shift, scale, norm_g, w_in, gq, gk, dgq, dgk, g64, g32, cos_t, sin_t)


def _rope_tables(s):
    pos = jnp.arange(s)
    rows = (pos // GRID_W).astype(F32)
    cols = (pos % GRID_W).astype(F32)
    m = DIFF_QK_DIM // 2
    inv = ROPE_BASE ** (-jnp.arange(0, m, 2, dtype=F32) / m)
    ang_r = rows[:, None] * inv[None, :]
    ang_c = cols[:, None] * inv[None, :]
    ang = jnp.concatenate([ang_r, ang_r, ang_c, ang_c], axis=1)
    sign = jnp.asarray(np.tile(np.repeat([-1.0, 1.0], 8), 2), F32)
    cos32 = jnp.cos(ang)
    sin32 = jnp.sin(ang) * sign[None, :]
    reps = DIFF_WIDTH // DIFF_QK_DIM
    return jnp.tile(cos32, (1, reps)), jnp.tile(sin32, (1, reps))


NA_GROUP_ROWS = 8
NA_HALF = NA_WIDTH // 2
NA_HEADS_PER_HALF = NA_HALF // HEAD_DIM


def _head_block_mask(rows_per_head, n_heads, width):
    r = lax.broadcasted_iota(jnp.int32, (rows_per_head * n_heads, width), 0) // rows_per_head
    c = lax.broadcasted_iota(jnp.int32, (rows_per_head * n_heads, width), 1) // (width // n_heads)
    return r == c


def _na_kernel(q_ref, kp_ref, k0_ref, kn_ref, vp_ref, v0_ref, vn_ref, kc_ref, vc_ref, bias_ref,
               o_ref, kbuf, vbuf, s_a, s_b):
    g = pl.program_id(1)
    blk = NA_GROUP_ROWS * GRID_W
    kbuf[0:blk] = kp_ref[0]
    kbuf[blk:2 * blk] = k0_ref[0]
    kbuf[2 * blk:3 * blk] = kn_ref[0]
    vbuf[0:blk] = vp_ref[0]
    vbuf[blk:2 * blk] = v0_ref[0]
    vbuf[2 * blk:3 * blk] = vn_ref[0]
    n_rows = pl.num_programs(1) * NA_GROUP_ROWS
    head_mask = _head_block_mask(GRID_W, NA_HEADS_PER_HALF, NA_HALF)
    n_keys = NA_KH * GRID_W

    s_bufs = (s_a, s_b)

    def window(i):
        r = g * NA_GROUP_ROWS + i
        row_start = jnp.clip(r - NA_KH // 2, 0, n_rows - NA_KH)
        off = pl.multiple_of((row_start - g * NA_GROUP_ROWS + NA_GROUP_ROWS) * GRID_W, GRID_W)
        return off, row_start - r + (NA_KH - 1)

    def scores(t, s_buf):
        i, hh = divmod(t, 2)
        off, d0 = window(i)
        cs = slice(hh * NA_HALF, (hh + 1) * NA_HALF)
        qh = q_ref[0, i * GRID_W:(i + 1) * GRID_W, cs]
        qst = jnp.concatenate([qh] * NA_HEADS_PER_HALF, axis=0)
        qst = jnp.where(head_mask, qst, jnp.zeros_like(qst))
        bias = jnp.concatenate([bias_ref[d0 + 2 * j, hh] for j in range(NA_KH // 2)], axis=1)
        s_buf[:, 0:n_keys] = _dot_nt(qst, kbuf[pl.ds(off, n_keys), cs]) + bias
        s_buf[:, n_keys:] = _dot_nt(qst, kc_ref[0, :, cs])

    def finish(t, s_buf):
        i, hh = divmod(t, 2)
        off, _ = window(i)
        cs = slice(hh * NA_HALF, (hh + 1) * NA_HALF)
        s = s_buf[...]
        m = jnp.max(s, axis=-1, keepdims=True)
        e = jnp.exp(s - m)
        l = jnp.sum(e, axis=-1, keepdims=True)
        eb = e.astype(BF16)
        o = _dot(eb[:, 0:n_keys], vbuf[pl.ds(off, n_keys), cs]) + _dot(eb[:, n_keys:], vc_ref[0, :, cs])
        o = jnp.where(head_mask, o * (1.0 / l), 0.0)
        out = o[0:GRID_W]
        for hl in range(1, NA_HEADS_PER_HALF):
            out = out + o[hl * GRID_W:(hl + 1) * GRID_W]
        o_ref[0, i * GRID_W:(i + 1) * GRID_W, cs] = out.astype(o_ref.dtype)

    n_jobs = 2 * NA_GROUP_ROWS
    scores(0, s_a)
    for t in range(n_jobs):
        if t + 1 < n_jobs:
            scores(t + 1, s_bufs[(t + 1) % 2])
        finish(t, s_bufs[t % 2])


def _na_bias_table(rpb):
    h, n_dr, _ = rpb.shape
    qc = np.arange(GRID_W)
    kc = np.arange(GRID_W)
    col_start = np.clip(qc - NA_KW // 2, 0, GRID_W - NA_KW)
    valid = (kc[None, :] >= col_start[:, None]) & (kc[None, :] < col_start[:, None] + NA_KW)
    pad = GRID_W - NA_KW
    period = 2 * GRID_W - 1
    rext = jnp.pad(rpb, ((0, 0), (0, 0), (pad, pad)))
    flat = jnp.tile(rext, (1, 1, GRID_W + 1))[:, :, :GRID_W * (period + 1)]
    hankel = flat.reshape(h, n_dr, GRID_W, period + 1)[..., :GRID_W]
    toep = hankel[:, :, ::-1, :]
    toep = jnp.where(jnp.asarray(valid)[None, None], toep, NEG_BIG)
    pair = jnp.concatenate([toep[:, :-1], toep[:, 1:]], axis=-1)
    pair = pair.reshape(2, NA_HEADS_PER_HALF, n_dr - 1, GRID_W, 2 * GRID_W).transpose(2, 0, 1, 3, 4)
    return pair.reshape(n_dr - 1, 2, NA_HEADS_PER_HALF * GRID_W, 2 * GRID_W).astype(F32)


def _neighbourhood_attention(q, k, v, kc, vc, rpb):
    b, s, _ = q.shape
    n_groups = s // (NA_GROUP_ROWS * GRID_W)
    blk = NA_GROUP_ROWS * GRID_W
    c = kc.shape[1]
    bias = _na_bias_table(rpb)
    cur = pl.BlockSpec((1, blk, NA_WIDTH), lambda bi, g: (bi, g, 0))
    prv = pl.BlockSpec((1, blk, NA_WIDTH), lambda bi, g: (bi, jnp.maximum(g - 1, 0), 0))
    nxt = pl.BlockSpec((1, blk, NA_WIDTH), lambda bi, g: (bi, jnp.minimum(g + 1, n_groups - 1), 0))
    ctx = pl.BlockSpec((1, c, NA_WIDTH), lambda bi, g: (bi, 0, 0))
    return pl.pallas_call(
        _na_kernel,
        grid=(b, n_groups),
        in_specs=[cur, prv, cur, nxt, prv, cur, nxt, ctx, ctx,
                  pl.BlockSpec(bias.shape, lambda bi, g: (0, 0, 0, 0))],
        out_specs=cur,
        out_shape=jax.ShapeDtypeStruct((b, s, NA_WIDTH), BF16),
        scratch_shapes=[pltpu.VMEM((3 * blk, NA_WIDTH), BF16), pltpu.VMEM((3 * blk, NA_WIDTH), BF16)]
                       + [pltpu.VMEM((NA_HEADS_PER_HALF * GRID_W, NA_KH * GRID_W + c), F32)] * 2,
        compiler_params=_params("parallel", "arbitrary"),
        name="neighbourhood_attention",
    )(q, k, k, k, v, v, v, kc, vc, bias)


def _dense_na_kernel(q_ref, k_ref, v_ref, o_ref):
    c = q_ref.shape[1]
    head_mask = _head_block_mask(c, NA_HEADS_PER_HALF, NA_HALF)
    for hh in range(2):
        cs = slice(hh * NA_HALF, (hh + 1) * NA_HALF)
        qh = q_ref[0, :, cs]
        qst = jnp.concatenate([qh] * NA_HEADS_PER_HALF, axis=0)
        qst = jnp.where(head_mask, qst, jnp.zeros_like(qst))
        s = _dot_nt(qst, k_ref[0, :, cs])
        m = jnp.max(s, axis=-1, keepdims=True)
        e = jnp.exp(s - m)
        l = jnp.sum(e, axis=-1, keepdims=True)
        o = _dot(e.astype(BF16), v_ref[0, :, cs])
        o = jnp.where(head_mask, o * (1.0 / l), 0.0)
        out = o[0:c]
        for hl in range(1, NA_HEADS_PER_HALF):
            out = out + o[hl * c:(hl + 1) * c]
        o_ref[0, :, cs] = out.astype(o_ref.dtype)


def _dense_na_attention(q, k, v):
    b, c, w = q.shape
    spec = pl.BlockSpec((1, c, w), lambda bi: (bi, 0, 0))
    return pl.pallas_call(
        _dense_na_kernel,
        grid=(b,),
        in_specs=[spec, spec, spec],
        out_specs=spec,
        out_shape=jax.ShapeDtypeStruct((b, c, w), BF16),
        compiler_params=_params("parallel"),
        name="context_dense_attention",
    )(q, k, v)


N_DIFF_STREAMS = 2 * DIFF_HEADS


DIFF_HEAD_PAIRS = DIFF_HEADS // 2
DIFF_MIN_TRUSTED_SUM = 2.0 ** -80


def _diff_kernel(*refs, lam_init, has_ctx):
    n_in = 7 if has_ctx else 5
    lam_ref, qt_ref, k_ref, vt_ref, kmax_ref = refs[:5]
    kc_ref, vct_ref = refs[5:7] if has_ctx else (None, None)
    o_ref, qs, s_a, s_b, m_scr, l_scr, acc = refs[n_in:]
    s_bufs = (s_a, s_b)
    tq = qt_ref.shape[2]
    n_chunks, _, tk = vt_ref.shape[1:]
    pair_w = 4 * tq

    qt = qt_ref[0]
    stream = lax.broadcasted_iota(jnp.int32, qt.shape, 0) // DIFF_QK_DIM
    for st in range(N_DIFF_STREAMS):
        qs[:, st * tq:(st + 1) * tq] = jnp.where(stream == st, qt, jnp.zeros_like(qt))
    bound = jnp.sum(jnp.abs(qs[...].astype(F32)) * kmax_ref[0], axis=0, keepdims=True)
    m_scr[...] = bound * (1.0 + 2.0 ** -10) + 2.0 ** -10
    l_scr[...] = jnp.zeros(l_scr.shape, F32)
    acc[...] = jnp.zeros(acc.shape, F32)

    def scores(kk, s_buf):
        n = kk.shape[0]
        for pair in range(DIFF_HEAD_PAIRS):
            cols = slice(pair * pair_w, (pair + 1) * pair_w)
            s_buf[0:n, cols] = _dot(kk, qs[:, cols])

    def absorb_bounded(s_buf, n, vt_rows):
        for pair in range(DIFF_HEAD_PAIRS):
            cols = slice(pair * pair_w, (pair + 1) * pair_w)
            p = jnp.exp2(s_buf[0:n, cols] - m_scr[:, cols])
            l_scr[:, cols] += jnp.sum(p, axis=0, keepdims=True)
            pb = p.astype(BF16)
            for hh in range(2):
                h = 2 * pair + hh
                rows = slice(h * DIFF_V_DIM, (h + 1) * DIFF_V_DIM)
                hc = slice(hh * 2 * tq, (hh + 1) * 2 * tq)
                acc[rows, :] += _dot(vt_rows(rows), pb[:, hc])

    def absorb(s_buf, n, vt_rows):
        for pair in range(DIFF_HEAD_PAIRS):
            cols = slice(pair * pair_w, (pair + 1) * pair_w)
            s = s_buf[0:n, cols]
            m_old = m_scr[:, cols]
            m_new = jnp.maximum(m_old, jnp.max(s, axis=0, keepdims=True))
            alpha = jnp.exp2(m_old - m_new)
            p = jnp.exp2(s - m_new)
            l_scr[:, cols] = alpha * l_scr[:, cols] + jnp.sum(p, axis=0, keepdims=True)
            m_scr[:, cols] = m_new
            pb = p.astype(BF16)
            for hh in range(2):
                h = 2 * pair + hh
                rows = slice(h * DIFF_V_DIM, (h + 1) * DIFF_V_DIM)
                hc = slice(hh * 2 * tq, (hh + 1) * 2 * tq)
                acc[rows, :] = acc[rows, :] * alpha[:, hc] + _dot(vt_rows(rows), pb[:, hc])

    n_items = n_chunks + (1 if has_ctx else 0)

    def scores_item(t, buf):
        if has_ctx and isinstance(t, int) and t == n_chunks:
            scores(kc_ref[0], buf)
        else:
            start = t * tk if isinstance(t, int) else pl.multiple_of(t * tk, tk)
            scores(k_ref[0, pl.ds(start, tk), :], buf)

    def all_items(absorb_fn):
        def absorb_item(t, buf):
            if has_ctx and isinstance(t, int) and t == n_chunks:
                absorb_fn(buf, kc_ref.shape[1], lambda rows: vct_ref[0, rows, :])
            else:
                absorb_fn(buf, tk, lambda rows: vt_ref[0, t, rows, :])

        scores_item(0, s_a)
        n_pairs = (n_chunks - 1) // 2

        def body(i, carry):
            t = 2 * i
            scores_item(t + 1, s_b)
            absorb_item(t, s_a)
            scores_item(t + 2, s_a)
            absorb_item(t + 1, s_b)
            return carry

        lax.fori_loop(0, n_pairs, body, 0)
        for t in range(2 * n_pairs, n_items):
            if t + 1 < n_items:
                scores_item(t + 1, s_bufs[(t + 1) % 2])
            absorb_item(t, s_bufs[t % 2])

    all_items(absorb_bounded)

    @pl.when(jnp.min(l_scr[...]) < DIFF_MIN_TRUSTED_SUM)
    def _():
        m_scr[...] = jnp.full(m_scr.shape, NEG_BIG, F32)
        l_scr[...] = jnp.zeros(l_scr.shape, F32)
        acc[...] = jnp.zeros(acc.shape, F32)
        all_items(absorb)

    lp = lam_ref[...]
    lam = (jnp.exp(jnp.sum(lp[0:1] * lp[1:2], axis=1, keepdims=True))
           - jnp.exp(jnp.sum(lp[2:3] * lp[3:4], axis=1, keepdims=True)) + lam_init)
    inv_l = 1.0 / l_scr[...]
    outs = []
    for h in range(DIFF_HEADS):
        rows = slice(h * DIFF_V_DIM, (h + 1) * DIFF_V_DIM)
        o1 = acc[rows, 0:tq] * inv_l[:, 2 * h * tq:(2 * h + 1) * tq]
        o2 = acc[rows, tq:2 * tq] * inv_l[:, (2 * h + 1) * tq:(2 * h + 2) * tq]
        outs.append(o1 - lam * o2)
    o_ref[0] = jnp.concatenate(outs, axis=0).T


def _diff_attention(lam_p, qt, k, vt, kmax, lam_init, tq, ctx_kv=None):
    b, w, sq = qt.shape
    n_chunks, _, tk = vt.shape[1:]
    whole = lambda a: pl.BlockSpec((1,) + a.shape[1:], lambda bi, i: (bi,) + (0,) * (a.ndim - 1))
    operands = [lam_p, qt, k, vt, kmax]
    in_specs = [
        pl.BlockSpec(lam_p.shape, lambda bi, i: (0, 0)),
        pl.BlockSpec((1, w, tq), lambda bi, i: (bi, 0, i)),
        whole(k), whole(vt), whole(kmax),
    ]
    if ctx_kv is not None:
        operands += list(ctx_kv)
        in_specs += [whole(ctx_kv[0]), whole(ctx_kv[1])]
    return pl.pallas_call(
        functools.partial(_diff_kernel, lam_init=lam_init, has_ctx=ctx_kv is not None),
        grid=(b, sq // tq),
        in_specs=in_specs,
        out_specs=pl.BlockSpec((1, tq, w), lambda bi, i: (bi, i, 0)),
        out_shape=jax.ShapeDtypeStruct((b, sq, w), F32),
        scratch_shapes=[
            pltpu.VMEM((w, N_DIFF_STREAMS * tq), BF16),
            pltpu.VMEM((tk, N_DIFF_STREAMS * tq), F32),
            pltpu.VMEM((tk, N_DIFF_STREAMS * tq), F32),
            pltpu.VMEM((1, N_DIFF_STREAMS * tq), F32),
            pltpu.VMEM((1, N_DIFF_STREAMS * tq), F32),
            pltpu.VMEM((w, 2 * tq), F32),
        ],
        compiler_params=_params("parallel", "arbitrary"),
        name="differential_attention",
    )(*operands)


FFT_N1 = 64
FFT_N1_STEP = 8


def _dft_tables(n):
    idx = np.arange(n)
    ang = 2.0 * np.pi * ((idx[:, None] * idx[None, :]) % n) / n
    return np.cos(ang), np.sin(ang)


def _fft_stage_a_kernel(x_ref, t_ref, twc_ref, tws_ref, ur_ref, ui_ref):
    n2 = x_ref.shape[1]
    t = t_ref[...].astype(BF16)
    for i in range(FFT_N1_STEP):
        xs = x_ref[0, :, i * FNET_WIDTH:(i + 1) * FNET_WIDTH]
        u = _dot(t, xs)
        ur, ui = u[0:n2], u[n2:2 * n2]
        c = jnp.concatenate([twc_ref[i]] * (FNET_WIDTH // LANES), axis=1)
        s = jnp.concatenate([tws_ref[i]] * (FNET_WIDTH // LANES), axis=1)
        ur_ref[0, i] = (ur * c + ui * s).astype(BF16)
        ui_ref[0, i] = (ui * c - ur * s).astype(BF16)


def _fft_stage_c_kernel(ur_ref, ui_ref, c_ref, s_ref, pr_ref, pi_ref):
    ur = ur_ref[0]
    ui = ui_ref[0]
    c1 = c_ref[...].astype(BF16)
    s1 = s_ref[...].astype(BF16)
    pr_ref[0] = (_dot(c1, ur) + _dot(s1, ui)).astype(BF16)
    pi_ref[0] = (_dot(c1, ui) - _dot(s1, ur)).astype(BF16)


def _position_dft(fu):
    b, l, w = fu.shape
    n1, n2 = FFT_N1, l // FFT_N1
    scale = 1.0 / math.sqrt(l * FNET_GROUP_DIM)
    c2, s2 = _dft_tables(n2)
    t2 = jnp.asarray(np.concatenate([c2, -s2], axis=0) * scale, dtype=F32)
    ang = (2.0 * jnp.pi / l) * (jnp.arange(n1, dtype=F32)[:, None] * jnp.arange(n2, dtype=F32)[None, :])
    twc = jnp.broadcast_to(jnp.cos(ang)[:, :, None], (n1, n2, LANES))
    tws = jnp.broadcast_to(jnp.sin(ang)[:, :, None], (n1, n2, LANES))
    xv = fu.reshape(b, n2, n1 * w)
    sds = jax.ShapeDtypeStruct
    ur, ui = pl.pallas_call(
        _fft_stage_a_kernel,
        grid=(b, n1 // FFT_N1_STEP),
        in_specs=[
            pl.BlockSpec((1, n2, FFT_N1_STEP * w), lambda bi, j: (bi, 0, j)),
            pl.BlockSpec((2 * n2, n2), lambda bi, j: (0, 0)),
            pl.BlockSpec((FFT_N1_STEP, n2, LANES), lambda bi, j: (j, 0, 0)),
            pl.BlockSpec((FFT_N1_STEP, n2, LANES), lambda bi, j: (j, 0, 0)),
        ],
        out_specs=[pl.BlockSpec((1, FFT_N1_STEP, n2, w), lambda bi, j: (bi, j, 0, 0))] * 2,
        out_shape=[sds((b, n1, n2, w), BF16)] * 2,
        compiler_params=_params("parallel", "parallel"),
        name="position_dft_stage_a",
    )(xv, t2, twc, tws)
    c1, s1 = _dft_tables(n1)
    tn = 4096
    cols = n2 * w
    blk = pl.BlockSpec((1, n1, tn), lambda bi, j: (bi, 0, j))
    tab = pl.BlockSpec((n1, n1), lambda bi, j: (0, 0))
    pr, pi = pl.pallas_call(
        _fft_stage_c_kernel,
        grid=(b, cols // tn),
        in_specs=[blk, blk, tab, tab],
        out_specs=[blk, blk],
        out_shape=[sds((b, n1, cols), BF16)] * 2,
        compiler_params=_params("parallel", "parallel"),
        name="position_dft_stage_c",
    )(ur.reshape(b, n1, cols), ui.reshape(b, n1, cols), jnp.asarray(c1, F32), jnp.asarray(s1, F32))
    return pr.reshape(b, l, w), pi.reshape(b, l, w)


def _direct_dft_kernel(x_ref, t_ref, pr_ref, pi_ref):
    l = x_ref.shape[1]
    u = _dot(t_ref[...].astype(BF16), x_ref[0])
    pr_ref[0] = u[0:l].astype(BF16)
    pi_ref[0] = u[l:2 * l].astype(BF16)


def _position_dft_direct(fu):
    b, l, w = fu.shape
    scale = 1.0 / math.sqrt(l * FNET_GROUP_DIM)
    c, s = _dft_tables(l)
    t = jnp.asarray(np.concatenate([c, -s], axis=0) * scale, dtype=F32)
    spec = pl.BlockSpec((1, l, w), lambda bi: (bi, 0, 0))
    return pl.pallas_call(
        _direct_dft_kernel,
        grid=(b,),
        in_specs=[spec, pl.BlockSpec((2 * l, l), lambda bi: (0, 0))],
        out_specs=[spec, spec],
        out_shape=[jax.ShapeDtypeStruct((b, l, w), BF16)] * 2,
        compiler_params=_params("parallel"),
        name="context_position_dft",
    )(fu, t)


def _merge_kernel(na_ref, df_ref, pr_ref, pi_ref, x_ref, g1_ref, sh2_ref, sc2_ref, n2g_ref, subg_ref,
                  g64_ref, wc_ref, ws_ref, wf_ref, wo_ref, x1_ref, h2_ref, *, diff_scale):
    d = df_ref[0]
    gms = _dot((d * d).astype(BF16), g64_ref[...])
    dn = d * lax.rsqrt(gms + EPS) * subg_ref[...] * diff_scale
    fr = _dot(pr_ref[0], wc_ref[...].astype(BF16)) + _dot(pi_ref[0], ws_ref[...].astype(BF16))
    four = _dot(fr.astype(BF16), wf_ref[...])
    y = (_dot(na_ref[0], wo_ref[0:NA_WIDTH, :])
         + _dot(dn.astype(BF16), wo_ref[NA_WIDTH:NA_WIDTH + DIFF_WIDTH, :])
         + _dot(four.astype(BF16), wo_ref[NA_WIDTH + DIFF_WIDTH:, :]))
    x1 = x_ref[0] + g1_ref[0] * y
    x1_ref[0] = x1
    ms = jnp.mean(x1 * x1, axis=-1, keepdims=True)
    h = x1 * lax.rsqrt(ms + EPS) * n2g_ref[...]
    h2_ref[0] = (h * (1.0 + sc2_ref[0]) + sh2_ref[0]).astype(BF16)


def _channel_dft_tables():
    c, s = _dft_tables(FNET_GROUP_DIM)
    eye = np.eye(FNET_GROUPS)
    return jnp.asarray(np.kron(eye, c), F32), jnp.asarray(np.kron(eye, s), F32)


def _merge(na_o, diff_o, pr, pi, x, g1, sh2, sc2, norm2_g, subln_g, w_four, w_out, lam_init, tm):
    b, s, d = x.shape
    wc, ws = _channel_dft_tables()
    g64 = _group_mean_matrix(DIFF_WIDTH, DIFF_V_DIM)
    const = lambda shape: pl.BlockSpec(shape, lambda bi, i: (0,) * len(shape))
    per_b = pl.BlockSpec((1, 1, d), lambda bi, i: (bi, 0, 0))
    tok = lambda w: pl.BlockSpec((1, tm, w), lambda bi, i: (bi, i, 0))
    sq = (FNET_WIDTH, FNET_WIDTH)
    return pl.pallas_call(
        functools.partial(_merge_kernel, diff_scale=1.0 - lam_init),
        grid=(b, s // tm),
        in_specs=[tok(NA_WIDTH), tok(DIFF_WIDTH), tok(FNET_WIDTH), tok(FNET_WIDTH), tok(d),
                  per_b, per_b, per_b, const((1, d)), const((1, DIFF_WIDTH)),
                  const((DIFF_WIDTH, DIFF_WIDTH)), const(sq), const(sq), const(sq), const((d, d))],
        out_specs=[tok(d), tok(d)],
        out_shape=[jax.ShapeDtypeStruct((b, s, d), F32), jax.ShapeDtypeStruct((b, s, d), BF16)],
        compiler_params=_params("parallel", "parallel"),
        name="merge_projection",
    )(na_o, diff_o, pr, pi, x, g1, sh2, sc2, norm2_g, subln_g, g64, wc, ws, w_four, w_out)


FFN_TN = 256
FFN_HALO = BF16_SUBLANES


def _ffn_kernel(hp_ref, h_ref, hn_ref, wu_ref, cw_ref, cb_ref, wd_ref, x1_ref, g2_ref, o_ref,
                hcat, ug_a, uv_a, ug_b, uv_b, a_a, a_b, acc):
    i = pl.program_id(1)
    tm = h_ref.shape[1]
    nj = wd_ref.shape[0]
    hp = hp_ref[0]
    hn = hn_ref[0]
    hcat[0:FFN_HALO] = jnp.where(i > 0, hp, jnp.zeros_like(hp))
    hcat[FFN_HALO:FFN_HALO + tm] = h_ref[0]
    hcat[FFN_HALO + tm:] = jnp.where(i < pl.num_programs(1) - 1, hn, jnp.zeros_like(hn))
    acc[...] = jnp.zeros(acc.shape, F32)
    bufs = ((ug_a, uv_a), (ug_b, uv_b))
    a_bufs = (a_a, a_b)

    def project_up(j, buf):
        buf[0][...] = _dot(hcat[...], wu_ref[j])
        buf[1][...] = _dot(hcat[...], wu_ref[nj + j])

    def conv(u, cw, cb):
        total = u.shape[0]
        prv = pltpu.roll(u, 1, axis=0)[FFN_HALO:FFN_HALO + tm]
        cur = u[FFN_HALO:FFN_HALO + tm]
        nxt = pltpu.roll(u, total - 1, axis=0)[FFN_HALO:FFN_HALO + tm]
        return prv * cw[0:1] + cur * cw[1:2] + nxt * cw[2:3] + cb

    def gate(j, buf, a_buf):
        g = conv(buf[0][...], cw_ref[j], cb_ref[j])
        val = conv(buf[1][...], cw_ref[nj + j], cb_ref[nj + j])
        a_buf[...] = (g * _sigmoid(g) * val).astype(BF16)

    def step(t, par, do_up, do_gate, do_down):
        if do_up:
            project_up(t + 1, bufs[1 - par])
        if do_gate:
            gate(t, bufs[par], a_bufs[par])
        if do_down:
            acc[...] += _dot(a_bufs[1 - par][...], wd_ref[t - 1])

    project_up(0, bufs[0])
    for t in range(nj + 1):
        step(t, t % 2, t + 1 < nj, t < nj, t > 0)
    o_ref[0] = x1_ref[0] + g2_ref[0] * acc[...]


def _conv_ffn(h2, x1, g2, w_up, conv_w, conv_b, w_down, tm):
    b, s, d = x1.shape
    nj = w_down.shape[0]
    per_tile = tm // FFN_HALO
    n_halo_blocks = s // FFN_HALO
    rows = tm + 2 * FFN_HALO
    resident = lambda a: pl.BlockSpec(a.shape, lambda bi, i: (0,) * a.ndim, pipeline_mode=pl.Buffered(1))
    return pl.pallas_call(
        _ffn_kernel,
        grid=(b, s // tm),
        in_specs=[
            pl.BlockSpec((1, FFN_HALO, d), lambda bi, i: (bi, jnp.maximum(i * per_tile - 1, 0), 0)),
            pl.BlockSpec((1, tm, d), lambda bi, i: (bi, i, 0)),
            pl.BlockSpec((1, FFN_HALO, d), lambda bi, i: (bi, jnp.minimum((i + 1) * per_tile, n_halo_blocks - 1), 0)),
            resident(w_up), resident(conv_w), resident(conv_b), resident(w_down),
            pl.BlockSpec((1, tm, d), lambda bi, i: (bi, i, 0)),
            pl.BlockSpec((1, 1, d), lambda bi, i: (bi, 0, 0)),
        ],
        out_specs=pl.BlockSpec((1, tm, d), lambda bi, i: (bi, i, 0)),
        out_shape=jax.ShapeDtypeStruct((b, s, d), F32),
        scratch_shapes=[pltpu.VMEM((rows, d), BF16)] + [pltpu.VMEM((rows, FFN_TN), F32)] * 4
                       + [pltpu.VMEM((tm, FFN_TN), BF16)] * 2 + [pltpu.VMEM((tm, d), F32)],
        compiler_params=_params("parallel", "arbitrary"),
        name="conv_ffn",
    )(h2, h2, h2, w_up, conv_w, conv_b, w_down, x1, g2)


def kernel(x, c, ctx, c_ctx, w_mod, b_mod, norm1_g, w_in, na_q_g, na_k_g, na_rpb, diff_q_g, diff_k_g,
           diff_lambda, diff_subln_g, w_fourier, w_out, norm2_g, w_up, conv_w, conv_b, w_down):
    b, s, d = x.shape
    n_ctx = ctx.shape[1]
    depth = w_mod.shape[0]

    cc = jnp.zeros((8, d), F32).at[0:b].set(c).at[b].set(c_ctx)
    mod = _modulation(cc, w_mod, b_mod)

    cos_x, sin_x = _rope_tables(s)
    cos_c = jnp.ones((n_ctx, DIFF_WIDTH), F32)
    sin_c = jnp.zeros((n_ctx, DIFF_WIDTH), F32)

    cx = ctx
    for l in range(depth):
        lam_init = 0.8 - 0.6 * math.exp(-0.3 * l)
        mx = mod[l, 0:b].reshape(b, 1, 6, d)
        mc = jnp.broadcast_to(mod[l, b].reshape(1, 1, 6, d), (b, 1, 6, d))
        sh1, sc1, g1, sh2, sc2, g2 = (mx[:, :, t] for t in range(6))
        csh1, csc1, cg1, csh2, csc2, cg2 = (mc[:, :, t] for t in range(6))

        w_in_b = w_in[l].astype(BF16)
        w_out_b = w_out[l].astype(BF16)
        w_four_b = w_fourier[l].astype(BF16)
        n_up = w_up.shape[2] // FFN_TN
        w_up_b = w_up[l].astype(BF16).reshape(d, n_up, FFN_TN).transpose(1, 0, 2)
        w_down_b = w_down[l].astype(BF16).reshape(n_up // 2, FFN_TN, d)
        cw = conv_w[l].reshape(3, n_up, FFN_TN).transpose(1, 0, 2)
        n1g = norm1_g[l].reshape(1, d)
        n2g = norm2_g[l].reshape(1, d)
        gq = jnp.tile(na_q_g[l], NA_HEADS).reshape(1, NA_WIDTH)
        gk = jnp.tile(na_k_g[l], NA_HEADS).reshape(1, NA_WIDTH)
        dgq = jnp.tile(diff_q_g[l], 2 * DIFF_HEADS).reshape(1, DIFF_WIDTH)
        dgk = jnp.tile(diff_k_g[l], 2 * DIFF_HEADS).reshape(1, DIFF_WIDTH)
        subg = jnp.tile(diff_subln_g[l], DIFF_HEADS).reshape(1, DIFF_WIDTH)
        cb = conv_b[l].reshape(n_up, 1, FFN_TN)

        naq, nak, nav, dqt, dk, dvt, fu, kabs = _in_projection(
            x, sh1, sc1, n1g, w_in_b, gq, gk, dgq, dgk, cos_x, sin_x, tm=1024)
        cnaq, cnak, cnav, cdqt, cdk, cdvt, cfu, ckabs = _in_projection(
            cx, csh1, csc1, n1g, w_in_b, gq, gk, dgq, dgk, cos_c, sin_c, tm=n_ctx)
        kmax_c = jnp.max(ckabs, axis=1).reshape(b, DIFF_WIDTH, 1)
        kmax_x = jnp.maximum(jnp.max(kabs, axis=1).reshape(b, DIFF_WIDTH, 1), kmax_c)

        na_x = _neighbourhood_attention(naq, nak, nav, cnak, cnav, na_rpb[l])
        diff_x = _diff_attention(diff_lambda[l], dqt, dk, dvt, kmax_x, lam_init, tq=128,
                                 ctx_kv=(cdk, cdvt.reshape(b, DIFF_WIDTH, n_ctx)))
        pr, pi = _position_dft(fu)
        x1, h2 = _merge(na_x, diff_x, pr, pi, x, g1, sh2, sc2, n2g, subg, w_four_b, w_out_b, lam_init, tm=512)
        x = _conv_ffn(h2, x1, g2, w_up_b, cw, cb, w_down_b, tm=512)

        if l < depth - 1:
            na_c = _dense_na_attention(cnaq, cnak, cnav)
            diff_c = _diff_attention(diff_lambda[l], cdqt, cdk, cdvt, kmax_c, lam_init, tq=128)
            cpr, cpi = _position_dft_direct(cfu)
            cx1, ch2 = _merge(na_c, diff_c, cpr, cpi, cx, cg1, csh2, csc2, n2g, subg, w_four_b, w_out_b,
                              lam_init, tm=n_ctx)
            cx = _conv_ffn(ch2, cx1, cg2, w_up_b, cw, cb, w_down_b, tm=n_ctx)
    return x
```

```python
import functools
import math

import numpy as np
import jax
import jax.numpy as jnp
from jax import lax
from jax.experimental import pallas as pl
from jax.experimental.pallas import tpu as pltpu

F32 = jnp.float32
BF16 = jnp.bfloat16

GRID_W = 64
EPS = 1e-6
ROPE_BASE = 10000.0
HEAD_DIM = 64
NA_HEADS = 8
NA_WIDTH = NA_HEADS * HEAD_DIM
NA_KH = 8
NA_KW = 16
DIFF_HEADS = 4
DIFF_QK_DIM = 32
DIFF_V_DIM = 64
DIFF_WIDTH = DIFF_HEADS * DIFF_V_DIM
FNET_GROUPS = 4
FNET_GROUP_DIM = 64
FNET_WIDTH = FNET_GROUPS * FNET_GROUP_DIM
NA_Q0 = 0
NA_K0 = NA_Q0 + NA_WIDTH
NA_V0 = NA_K0 + NA_WIDTH
DQ0 = NA_V0 + NA_WIDTH
DK0 = DQ0 + DIFF_WIDTH
DV0 = DK0 + DIFF_WIDTH
FN0 = DV0 + DIFF_WIDTH
IN_WIDTH = FN0 + FNET_WIDTH
NEG_BIG = -1e30
LOG2E = 1.4426950408889634

VMEM_LIMIT_BYTES = 52 * 1024 * 1024
LANES = 128
BF16_SUBLANES = 16


def _params(*sem, flags=None):
    return pltpu.CompilerParams(dimension_semantics=sem, vmem_limit_bytes=VMEM_LIMIT_BYTES, flags=flags)


def _dot(a, b):
    return jnp.dot(a, b, preferred_element_type=F32)


def _dot_nt(a, b):
    return lax.dot_general(a, b, (((1,), (1,)), ((), ())), preferred_element_type=F32)


def _sigmoid(x):
    return 1.0 / (1.0 + jnp.exp(-x))


def _group_mean_matrix(width, group):
    idx = np.arange(width) // group
    return jnp.asarray((idx[:, None] == idx[None, :]).astype(np.float32) / group, dtype=BF16)


def _mod_kernel(c_ref, w_ref, b_ref, o_ref):
    c = c_ref[...]
    s = c * _sigmoid(c)
    s_hi = s.astype(BF16)
    s_lo = (s - s_hi.astype(F32)).astype(BF16)
    w = w_ref[0]
    w_hi = w.astype(BF16)
    w_lo = (w - w_hi.astype(F32)).astype(BF16)
    o_ref[0] = _dot(s_hi, w_hi) + _dot(s_hi, w_lo) + _dot(s_lo, w_hi) + b_ref[0]


def _modulation(cc, w_mod, b_mod):
    depth, d, n = w_mod.shape
    tn = 768
    return pl.pallas_call(
        _mod_kernel,
        grid=(depth, n // tn),
        in_specs=[
            pl.BlockSpec((8, d), lambda l, j: (0, 0)),
            pl.BlockSpec((1, d, tn), lambda l, j: (l, 0, j)),
            pl.BlockSpec((1, 1, tn), lambda l, j: (l, 0, j)),
        ],
        out_specs=pl.BlockSpec((1, 8, tn), lambda l, j: (l, 0, j)),
        out_shape=jax.ShapeDtypeStruct((depth, 8, n), F32),
        compiler_params=_params("parallel", "parallel"),
        name="modulation",
    )(cc, w_mod, b_mod.reshape(depth, 1, n))


def _inproj_kernel(x_ref, sh_ref, sc_ref, g_ref, w_ref, gq_ref, gk_ref, dgq_ref, dgk_ref,
                   g64_ref, g32_ref, cos_ref, sin_ref,
                   naq_ref, nak_ref, nav_ref, dqt_ref, dk_ref, dvt_ref, fu_ref, kabs_ref):
    x = x_ref[0]
    ms = jnp.mean(x * x, axis=-1, keepdims=True)
    h = x * lax.rsqrt(ms + EPS) * g_ref[...]
    h = h * (1.0 + sc_ref[0]) + sh_ref[0]
    hb = h.astype(BF16)

    def proj(c0, c1):
        return _dot(hb, w_ref[:, c0:c1])

    def group_norm(p, gmat_ref, gain):
        gms = _dot((p * p).astype(BF16), gmat_ref[...])
        return p * lax.rsqrt(gms + EPS) * gain

    def rope(y):
        lane = lax.broadcasted_iota(jnp.int32, y.shape, 1)
        nxt = pltpu.roll(y, DIFF_WIDTH - 8, axis=1)
        prv = pltpu.roll(y, 8, axis=1)
        partner = jnp.where((lane & 8) == 0, nxt, prv)
        return y * cos_ref[...] + partner * sin_ref[...]

    q = group_norm(proj(NA_Q0, NA_K0), g64_ref, gq_ref[...]) * (HEAD_DIM ** -0.5)
    naq_ref[0] = q.astype(BF16)
    k = group_norm(proj(NA_K0, NA_V0), g64_ref, gk_ref[...])
    nak_ref[0] = k.astype(BF16)
    nav_ref[0] = proj(NA_V0, DQ0).astype(BF16)
    dq = rope(group_norm(proj(DQ0, DK0), g32_ref, dgq_ref[...])) * (DIFF_QK_DIM ** -0.5 * LOG2E)
    dqt_ref[0] = dq.T.astype(BF16)
    dk = rope(group_norm(proj(DK0, DV0), g32_ref, dgk_ref[...]))
    dkb = dk.astype(BF16)
    dk_ref[0] = dkb
    kabs_ref[0, 0] = jnp.max(jnp.abs(dkb.astype(F32)), axis=0, keepdims=True)
    dvt_ref[0, 0] = proj(DV0, FN0).T.astype(BF16)
    fu_ref[0] = proj(FN0, IN_WIDTH).astype(BF16)


def _in_projection(x, shift, scale, norm_g, w_in, gq, gk, dgq, dgk, cos_t, sin_t, tm):
    b, s, d = x.shape
    g64 = _group_mean_matrix(NA_WIDTH, HEAD_DIM)
    g32 = _group_mean_matrix(DIFF_WIDTH, DIFF_QK_DIM)
    const = lambda shape: pl.BlockSpec(shape, lambda bi, i: (0,) * len(shape))
    per_b = pl.BlockSpec((1, 1, d), lambda bi, i: (bi, 0, 0))
    tok = lambda w: pl.BlockSpec((1, tm, w), lambda bi, i: (bi, i, 0))
    tok_t = lambda w: pl.BlockSpec((1, w, tm), lambda bi, i: (bi, 0, i))
    sds = jax.ShapeDtypeStruct
    return pl.pallas_call(
        _inproj_kernel,
        grid=(b, s // tm),
        in_specs=[
            tok(d), per_b, per_b, const((1, d)), const((d, IN_WIDTH)),
            const((1, NA_WIDTH)), const((1, NA_WIDTH)), const((1, DIFF_WIDTH)), const((1, DIFF_WIDTH)),
            const((NA_WIDTH, NA_WIDTH)), const((DIFF_WIDTH, DIFF_WIDTH)),
            pl.BlockSpec((tm, DIFF_WIDTH), lambda bi, i: (i, 0)),
            pl.BlockSpec((tm, DIFF_WIDTH), lambda bi, i: (i, 0)),
        ],
        out_specs=[tok(NA_WIDTH), tok(NA_WIDTH), tok(NA_WIDTH),
                   tok_t(DIFF_WIDTH), tok(DIFF_WIDTH),
                   pl.BlockSpec((1, 1, DIFF_WIDTH, tm), lambda bi, i: (bi, i, 0, 0)), tok(FNET_WIDTH),
                   pl.BlockSpec((1, 1, 1, DIFF_WIDTH), lambda bi, i: (bi, i, 0, 0))],
        out_shape=[sds((b, s, NA_WIDTH), BF16), sds((b, s, NA_WIDTH), BF16), sds((b, s, NA_WIDTH), BF16),
                   sds((b, DIFF_WIDTH, s), BF16), sds((b, s, DIFF_WIDTH), BF16),
                   sds((b, s // tm, DIFF_WIDTH, tm), BF16), sds((b, s, FNET_WIDTH), BF16),
                   sds((b, s // tm, 1, DIFF_WIDTH), F32)],
        compiler_params=_params("parallel", "parallel"),
        name="in_projection",
    )(x, shift, scale, norm_g, w_in, gq, gk, dgq, dgk, g64, g32, cos_t, sin_t)


def _rope_tables(s):
    pos = jnp.arange(s)
    rows = (pos // GRID_W).astype(F32)
    cols = (pos % GRID_W).astype(F32)
    m = DIFF_QK_DIM // 2
    inv = ROPE_BASE ** (-jnp.arange(0, m, 2, dtype=F32) / m)
    ang_r = rows[:, None] * inv[None, :]
    ang_c = cols[:, None] * inv[None, :]
    ang = jnp.concatenate([ang_r, ang_r, ang_c, ang_c], axis=1)
    sign = jnp.asarray(np.tile(np.repeat([-1.0, 1.0], 8), 2), F32)
    cos32 = jnp.cos(ang)
    sin32 = jnp.sin(ang) * sign[None, :]
    reps = DIFF_WIDTH // DIFF_QK_DIM
    return jnp.tile(cos32, (1, reps)), jnp.tile(sin32, (1, reps))


NA_GROUP_ROWS = 8
NA_HALF = NA_WIDTH // 2
NA_HEADS_PER_HALF = NA_HALF // HEAD_DIM


def _head_block_mask(rows_per_head, n_heads, width):
    r = lax.broadcasted_iota(jnp.int32, (rows_per_head * n_heads, width), 0) // rows_per_head
    c = lax.broadcasted_iota(jnp.int32, (rows_per_head * n_heads, width), 1) // (width // n_heads)
    return r == c


def _na_kernel(q_ref, kp_ref, k0_ref, kn_ref, vp_ref, v0_ref, vn_ref, kc_ref, vc_ref, bias_ref,
               o_ref, kbuf, vbuf, s_a, s_b):
    g = pl.program_id(1)
    blk = NA_GROUP_ROWS * GRID_W
    kbuf[0:blk] = kp_ref[0]
    kbuf[blk:2 * blk] = k0_ref[0]
    kbuf[2 * blk:3 * blk] = kn_ref[0]
    vbuf[0:blk] = vp_ref[0]
    vbuf[blk:2 * blk] = v0_ref[0]
    vbuf[2 * blk:3 * blk] = vn_ref[0]
    n_rows = pl.num_programs(1) * NA_GROUP_ROWS
    head_mask = _head_block_mask(GRID_W, NA_HEADS_PER_HALF, NA_HALF)
    n_keys = NA_KH * GRID_W

    s_bufs = (s_a, s_b)

    def window(i):
        r = g * NA_GROUP_ROWS + i
        row_start = jnp.clip(r - NA_KH // 2, 0, n_rows - NA_KH)
        off = pl.multiple_of((row_start - g * NA_GROUP_ROWS + NA_GROUP_ROWS) * GRID_W, GRID_W)
        return off, row_start - r + (NA_KH - 1)

    def scores(t, s_buf):
        i, hh = divmod(t, 2)
        off, d0 = window(i)
        cs = slice(hh * NA_HALF, (hh + 1) * NA_HALF)
        qh = q_ref[0, i * GRID_W:(i + 1) * GRID_W, cs]
        qst = jnp.concatenate([qh] * NA_HEADS_PER_HALF, axis=0)
        qst = jnp.where(head_mask, qst, jnp.zeros_like(qst))
        bias = jnp.concatenate([bias_ref[d0 + 2 * j, hh] for j in range(NA_KH // 2)], axis=1)
        s_buf[:, 0:n_keys] = _dot_nt(qst, kbuf[pl.ds(off, n_keys), cs]) + bias
        s_buf[:, n_keys:] = _dot_nt(qst, kc_ref[0, :, cs])

    def finish(t, s_buf):
        i, hh = divmod(t, 2)
        off, _ = window(i)
        cs = slice(hh * NA_HALF, (hh + 1) * NA_HALF)
        s = s_buf[...]
        m = jnp.max(s, axis=-1, keepdims=True)
        e = jnp.exp(s - m)
        l = jnp.sum(e, axis=-1, keepdims=True)
        eb = e.astype(BF16)
        o = _dot(eb[:, 0:n_keys], vbuf[pl.ds(off, n_keys), cs]) + _dot(eb[:, n_keys:], vc_ref[0, :, cs])
        o = jnp.where(head_mask, o * (1.0 / l), 0.0)
        out = o[0:GRID_W]
        for hl in range(1, NA_HEADS_PER_HALF):
            out = out + o[hl * GRID_W:(hl + 1) * GRID_W]
        o_ref[0, i * GRID_W:(i + 1) * GRID_W, cs] = out.astype(o_ref.dtype)

    n_jobs = 2 * NA_GROUP_ROWS
    scores(0, s_a)
    for t in range(n_jobs):
        if t + 1 < n_jobs:
            scores(t + 1, s_bufs[(t + 1) % 2])
        finish(t, s_bufs[t % 2])


def _na_bias_table(rpb):
    h, n_dr, _ = rpb.shape
    qc = np.arange(GRID_W)
    kc = np.arange(GRID_W)
    col_start = np.clip(qc - NA_KW // 2, 0, GRID_W - NA_KW)
    valid = (kc[None, :] >= col_start[:, None]) & (kc[None, :] < col_start[:, None] + NA_KW)
    pad = GRID_W - NA_KW
    period = 2 * GRID_W - 1
    rext = jnp.pad(rpb, ((0, 0), (0, 0), (pad, pad)))
    flat = jnp.tile(rext, (1, 1, GRID_W + 1))[:, :, :GRID_W * (period + 1)]
    hankel = flat.reshape(h, n_dr, GRID_W, period + 1)[..., :GRID_W]
    toep = hankel[:, :, ::-1, :]
    toep = jnp.where(jnp.asarray(valid)[None, None], toep, NEG_BIG)
    pair = jnp.concatenate([toep[:, :-1], toep[:, 1:]], axis=-1)
    pair = pair.reshape(2, NA_HEADS_PER_HALF, n_dr - 1, GRID_W, 2 * GRID_W).transpose(2, 0, 1, 3, 4)
    return pair.reshape(n_dr - 1, 2, NA_HEADS_PER_HALF * GRID_W, 2 * GRID_W).astype(F32)


def _neighbourhood_attention(q, k, v, kc, vc, rpb):
    b, s, _ = q.shape
    n_groups = s // (NA_GROUP_ROWS * GRID_W)
    blk = NA_GROUP_ROWS * GRID_W
    c = kc.shape[1]
    bias = _na_bias_table(rpb)
    cur = pl.BlockSpec((1, blk, NA_WIDTH), lambda bi, g: (bi, g, 0))
    prv = pl.BlockSpec((1, blk, NA_WIDTH), lambda bi, g: (bi, jnp.maximum(g - 1, 0), 0))
    nxt = pl.BlockSpec((1, blk, NA_WIDTH), lambda bi, g: (bi, jnp.minimum(g + 1, n_groups - 1), 0))
    ctx = pl.BlockSpec((1, c, NA_WIDTH), lambda bi, g: (bi, 0, 0))
    return pl.pallas_call(
        _na_kernel,
        grid=(b, n_groups),
        in_specs=[cur, prv, cur, nxt, prv, cur, nxt, ctx, ctx,
                  pl.BlockSpec(bias.shape, lambda bi, g: (0, 0, 0, 0))],
        out_specs=cur,
        out_shape=jax.ShapeDtypeStruct((b, s, NA_WIDTH), BF16),
        scratch_shapes=[pltpu.VMEM((3 * blk, NA_WIDTH), BF16), pltpu.VMEM((3 * blk, NA_WIDTH), BF16)]
                       + [pltpu.VMEM((NA_HEADS_PER_HALF * GRID_W, NA_KH * GRID_W + c), F32)] * 2,
        compiler_params=_params("parallel", "arbitrary"),
        name="neighbourhood_attention",
    )(q, k, k, k, v, v, v, kc, vc, bias)


def _dense_na_kernel(q_ref, k_ref, v_ref, o_ref):
    c = q_ref.shape[1]
    head_mask = _head_block_mask(c, NA_HEADS_PER_HALF, NA_HALF)
    for hh in range(2):
        cs = slice(hh * NA_HALF, (hh + 1) * NA_HALF)
        qh = q_ref[0, :, cs]
        qst = jnp.concatenate([qh] * NA_HEADS_PER_HALF, axis=0)
        qst = jnp.where(head_mask, qst, jnp.zeros_like(qst))
        s = _dot_nt(qst, k_ref[0, :, cs])
        m = jnp.max(s, axis=-1, keepdims=True)
        e = jnp.exp(s - m)
        l = jnp.sum(e, axis=-1, keepdims=True)
        o = _dot(e.astype(BF16), v_ref[0, :, cs])
        o = jnp.where(head_mask, o * (1.0 / l), 0.0)
        out = o[0:c]
        for hl in range(1, NA_HEADS_PER_HALF):
            out = out + o[hl * c:(hl + 1) * c]
        o_ref[0, :, cs] = out.astype(o_ref.dtype)


def _dense_na_attention(q, k, v):
    b, c, w = q.shape
    spec = pl.BlockSpec((1, c, w), lambda bi: (bi, 0, 0))
    return pl.pallas_call(
        _dense_na_kernel,
        grid=(b,),
        in_specs=[spec, spec, spec],
        out_specs=spec,
        out_shape=jax.ShapeDtypeStruct((b, c, w), BF16),
        compiler_params=_params("parallel"),
        name="context_dense_attention",
    )(q, k, v)


N_DIFF_STREAMS = 2 * DIFF_HEADS


DIFF_HEAD_PAIRS = DIFF_HEADS // 2
DIFF_MIN_TRUSTED_SUM = 2.0 ** -80


def _diff_kernel(*refs, lam_init, has_ctx):
    n_in = 7 if has_ctx else 5
    lam_ref, qt_ref, k_ref, vt_ref, kmax_ref = refs[:5]
    kc_ref, vct_ref = refs[5:7] if has_ctx else (None, None)
    o_ref, qs, s_a, s_b, p_a, p_b, m_scr, l_scr, acc = refs[n_in:]
    tq = qt_ref.shape[2]
    n_chunks, _, tk = vt_ref.shape[1:]
    pair_w = 4 * tq

    qt = qt_ref[0]
    stream = lax.broadcasted_iota(jnp.int32, qt.shape, 0) // DIFF_QK_DIM
    for st in range(N_DIFF_STREAMS):
        qs[:, st * tq:(st + 1) * tq] = jnp.where(stream == st, qt, jnp.zeros_like(qt))
    bound = jnp.sum(jnp.abs(qs[...].astype(F32)) * kmax_ref[0], axis=0, keepdims=True)
    m_scr[...] = bound * (1.0 + 2.0 ** -10) + 2.0 ** -10
    l_scr[...] = jnp.zeros(l_scr.shape, F32)
    acc[...] = jnp.zeros(acc.shape, F32)

    def scores(kk, s_buf):
        n = kk.shape[0]
        for pair in range(DIFF_HEAD_PAIRS):
            cols = slice(pair * pair_w, (pair + 1) * pair_w)
            s_buf[0:n, cols] = _dot(kk, qs[:, cols])

    def probs_bounded(kk, p_buf):
        n = kk.shape[0]
        for pair in range(DIFF_HEAD_PAIRS):
            cols = slice(pair * pair_w, (pair + 1) * pair_w)
            p = jnp.exp2(_dot(kk, qs[:, cols]) - m_scr[:, cols])
            l_scr[:, cols] += jnp.sum(p, axis=0, keepdims=True)
            p_buf[0:n, cols] = p.astype(BF16)

    def values_bounded(p_buf, n, vt_rows):
        for h in range(DIFF_HEADS):
            rows = slice(h * DIFF_V_DIM, (h + 1) * DIFF_V_DIM)
            cols = slice(2 * h * tq, (2 * h + 2) * tq)
            acc[rows, :] += _dot(vt_rows(rows), p_buf[0:n, cols])

    def absorb(s_buf, n, vt_rows):
        for pair in range(DIFF_HEAD_PAIRS):
            cols = slice(pair * pair_w, (pair + 1) * pair_w)
            s = s_buf[0:n, cols]
            m_old = m_scr[:, cols]
            m_new = jnp.maximum(m_old, jnp.max(s, axis=0, keepdims=True))
            alpha = jnp.exp2(m_old - m_new)
            p = jnp.exp2(s - m_new)
            l_scr[:, cols] = alpha * l_scr[:, cols] + jnp.sum(p, axis=0, keepdims=True)
            m_scr[:, cols] = m_new
            pb = p.astype(BF16)
            for hh in range(2):
                h = 2 * pair + hh
                rows = slice(h * DIFF_V_DIM, (h + 1) * DIFF_V_DIM)
                hc = slice(hh * 2 * tq, (hh + 1) * 2 * tq)
                acc[rows, :] = acc[rows, :] * alpha[:, hc] + _dot(vt_rows(rows), pb[:, hc])

    n_items = n_chunks + (1 if has_ctx else 0)

    def all_items(scores_fn, absorb_fn, bufs):
        def scores_item(t, buf):
            if has_ctx and isinstance(t, int) and t == n_chunks:
                scores_fn(kc_ref[0], buf)
            else:
                start = t * tk if isinstance(t, int) else pl.multiple_of(t * tk, tk)
                scores_fn(k_ref[0, pl.ds(start, tk), :], buf)

        def absorb_item(t, buf):
            if has_ctx and isinstance(t, int) and t == n_chunks:
                absorb_fn(buf, kc_ref.shape[1], lambda rows: vct_ref[0, rows, :])
            else:
                absorb_fn(buf, tk, lambda rows: vt_ref[0, t, rows, :])

        scores_item(0, bufs[0])
        n_pairs = (n_chunks - 1) // 2

        def body(i, carry):
            t = 2 * i
            scores_item(t + 1, bufs[1])
            absorb_item(t, bufs[0])
            scores_item(t + 2, bufs[0])
            absorb_item(t + 1, bufs[1])
            return carry

        lax.fori_loop(0, n_pairs, body, 0)
        for t in range(2 * n_pairs, n_items):
            if t + 1 < n_items:
                scores_item(t + 1, bufs[(t + 1) % 2])
            absorb_item(t, bufs[t % 2])

    all_items(probs_bounded, values_bounded, (p_a, p_b))

    @pl.when(jnp.min(l_scr[...]) < DIFF_MIN_TRUSTED_SUM)
    def _():
        m_scr[...] = jnp.full(m_scr.shape, NEG_BIG, F32)
        l_scr[...] = jnp.zeros(l_scr.shape, F32)
        acc[...] = jnp.zeros(acc.shape, F32)
        all_items(scores, absorb, (s_a, s_b))

    lp = lam_ref[...]
    lam = (jnp.exp(jnp.sum(lp[0:1] * lp[1:2], axis=1, keepdims=True))
           - jnp.exp(jnp.sum(lp[2:3] * lp[3:4], axis=1, keepdims=True)) + lam_init)
    inv_l = 1.0 / l_scr[...]
    outs = []
    for h in range(DIFF_HEADS):
        rows = slice(h * DIFF_V_DIM, (h + 1) * DIFF_V_DIM)
        o1 = acc[rows, 0:tq] * inv_l[:, 2 * h * tq:(2 * h + 1) * tq]
        o2 = acc[rows, tq:2 * tq] * inv_l[:, (2 * h + 1) * tq:(2 * h + 2) * tq]
        outs.append(o1 - lam * o2)
    o_ref[0] = jnp.concatenate(outs, axis=0).T


def _diff_attention(lam_p, qt, k, vt, kmax, lam_init, tq, ctx_kv=None):
    b, w, sq = qt.shape
    n_chunks, _, tk = vt.shape[1:]
    whole = lambda a: pl.BlockSpec((1,) + a.shape[1:], lambda bi, i: (bi,) + (0,) * (a.ndim - 1))
    operands = [lam_p, qt, k, vt, kmax]
    in_specs = [
        pl.BlockSpec(lam_p.shape, lambda bi, i: (0, 0)),
        pl.BlockSpec((1, w, tq), lambda bi, i: (bi, 0, i)),
        whole(k), whole(vt), whole(kmax),
    ]
    if ctx_kv is not None:
        operands += list(ctx_kv)
        in_specs += [whole(ctx_kv[0]), whole(ctx_kv[1])]
    return pl.pallas_call(
        functools.partial(_diff_kernel, lam_init=lam_init, has_ctx=ctx_kv is not None),
        grid=(b, sq // tq),
        in_specs=in_specs,
        out_specs=pl.BlockSpec((1, tq, w), lambda bi, i: (bi, i, 0)),
        out_shape=jax.ShapeDtypeStruct((b, sq, w), F32),
        scratch_shapes=[
            pltpu.VMEM((w, N_DIFF_STREAMS * tq), BF16),
            pltpu.VMEM((tk, N_DIFF_STREAMS * tq), F32),
            pltpu.VMEM((tk, N_DIFF_STREAMS * tq), F32),
            pltpu.VMEM((tk, N_DIFF_STREAMS * tq), BF16),
            pltpu.VMEM((tk, N_DIFF_STREAMS * tq), BF16),
            pltpu.VMEM((1, N_DIFF_STREAMS * tq), F32),
            pltpu.VMEM((1, N_DIFF_STREAMS * tq), F32),
            pltpu.VMEM((w, 2 * tq), F32),
        ],
        compiler_params=_params("parallel", "arbitrary"),
        name="differential_attention",
    )(*operands)


FFT_N1 = 64
FFT_N1_STEP = 8


def _dft_tables(n):
    idx = np.arange(n)
    ang = 2.0 * np.pi * ((idx[:, None] * idx[None, :]) % n) / n
    return np.cos(ang), np.sin(ang)


def _fft_stage_a_kernel(x_ref, t_ref, twc_ref, tws_ref, ur_ref, ui_ref):
    n2 = x_ref.shape[1]
    t = t_ref[...].astype(BF16)
    for i in range(FFT_N1_STEP):
        xs = x_ref[0, :, i * FNET_WIDTH:(i + 1) * FNET_WIDTH]
        u = _dot(t, xs)
        ur, ui = u[0:n2], u[n2:2 * n2]
        c = jnp.concatenate([twc_ref[i]] * (FNET_WIDTH // LANES), axis=1)
        s = jnp.concatenate([tws_ref[i]] * (FNET_WIDTH // LANES), axis=1)
        ur_ref[0, i] = (ur * c + ui * s).astype(BF16)
        ui_ref[0, i] = (ui * c - ur * s).astype(BF16)


def _fft_stage_c_kernel(ur_ref, ui_ref, c_ref, s_ref, pr_ref, pi_ref):
    ur = ur_ref[0]
    ui = ui_ref[0]
    c1 = c_ref[...].astype(BF16)
    s1 = s_ref[...].astype(BF16)
    pr_ref[0] = (_dot(c1, ur) + _dot(s1, ui)).astype(BF16)
    pi_ref[0] = (_dot(c1, ui) - _dot(s1, ur)).astype(BF16)


def _position_dft(fu):
    b, l, w = fu.shape
    n1, n2 = FFT_N1, l // FFT_N1
    scale = 1.0 / math.sqrt(l * FNET_GROUP_DIM)
    c2, s2 = _dft_tables(n2)
    t2 = jnp.asarray(np.concatenate([c2, -s2], axis=0) * scale, dtype=F32)
    ang = (2.0 * jnp.pi / l) * (jnp.arange(n1, dtype=F32)[:, None] * jnp.arange(n2, dtype=F32)[None, :])
    twc = jnp.broadcast_to(jnp.cos(ang)[:, :, None], (n1, n2, LANES))
    tws = jnp.broadcast_to(jnp.sin(ang)[:, :, None], (n1, n2, LANES))
    xv = fu.reshape(b, n2, n1 * w)
    sds = jax.ShapeDtypeStruct
    ur, ui = pl.pallas_call(
        _fft_stage_a_kernel,
        grid=(b, n1 // FFT_N1_STEP),
        in_specs=[
            pl.BlockSpec((1, n2, FFT_N1_STEP * w), lambda bi, j: (bi, 0, j)),
            pl.BlockSpec((2 * n2, n2), lambda bi, j: (0, 0)),
            pl.BlockSpec((FFT_N1_STEP, n2, LANES), lambda bi, j: (j, 0, 0)),
            pl.BlockSpec((FFT_N1_STEP, n2, LANES), lambda bi, j: (j, 0, 0)),
        ],
        out_specs=[pl.BlockSpec((1, FFT_N1_STEP, n2, w), lambda bi, j: (bi, j, 0, 0))] * 2,
        out_shape=[sds((b, n1, n2, w), BF16)] * 2,
        compiler_params=_params("parallel", "parallel"),
        name="position_dft_stage_a",
    )(xv, t2, twc, tws)
    c1, s1 = _dft_tables(n1)
    tn = 4096
    cols = n2 * w
    blk = pl.BlockSpec((1, n1, tn), lambda bi, j: (bi, 0, j))
    tab = pl.BlockSpec((n1, n1), lambda bi, j: (0, 0))
    pr, pi = pl.pallas_call(
        _fft_stage_c_kernel,
        grid=(b, cols // tn),
        in_specs=[blk, blk, tab, tab],
        out_specs=[blk, blk],
        out_shape=[sds((b, n1, cols), BF16)] * 2,
        compiler_params=_params("parallel", "parallel"),
        name="position_dft_stage_c",
    )(ur.reshape(b, n1, cols), ui.reshape(b, n1, cols), jnp.asarray(c1, F32), jnp.asarray(s1, F32))
    return pr.reshape(b, l, w), pi.reshape(b, l, w)


def _direct_dft_kernel(x_ref, t_ref, pr_ref, pi_ref):
    l = x_ref.shape[1]
    u = _dot(t_ref[...].astype(BF16), x_ref[0])
    pr_ref[0] = u[0:l].astype(BF16)
    pi_ref[0] = u[l:2 * l].astype(BF16)


def _position_dft_direct(fu):
    b, l, w = fu.shape
    scale = 1.0 / math.sqrt(l * FNET_GROUP_DIM)
    c, s = _dft_tables(l)
    t = jnp.asarray(np.concatenate([c, -s], axis=0) * scale, dtype=F32)
    spec = pl.BlockSpec((1, l, w), lambda bi: (bi, 0, 0))
    return pl.pallas_call(
        _direct_dft_kernel,
        grid=(b,),
        in_specs=[spec, pl.BlockSpec((2 * l, l), lambda bi: (0, 0))],
        out_specs=[spec, spec],
        out_shape=[jax.ShapeDtypeStruct((b, l, w), BF16)] * 2,
        compiler_params=_params("parallel"),
        name="context_position_dft",
    )(fu, t)


def _merge_kernel(na_ref, df_ref, pr_ref, pi_ref, x_ref, g1_ref, sh2_ref, sc2_ref, n2g_ref, subg_ref,
                  g64_ref, wc_ref, ws_ref, wf_ref, wo_ref, x1_ref, h2_ref, *, diff_scale):
    d = df_ref[0]
    gms = _dot((d * d).astype(BF16), g64_ref[...])
    dn = d * lax.rsqrt(gms + EPS) * subg_ref[...] * diff_scale
    fr = _dot(pr_ref[0], wc_ref[...].astype(BF16)) + _dot(pi_ref[0], ws_ref[...].astype(BF16))
    four = _dot(fr.astype(BF16), wf_ref[...])
    y = (_dot(na_ref[0], wo_ref[0:NA_WIDTH, :])
         + _dot(dn.astype(BF16), wo_ref[NA_WIDTH:NA_WIDTH + DIFF_WIDTH, :])
         + _dot(four.astype(BF16), wo_ref[NA_WIDTH + DIFF_WIDTH:, :]))
    x1 = x_ref[0] + g1_ref[0] * y
    x1_ref[0] = x1
    ms = jnp.mean(x1 * x1, axis=-1, keepdims=True)
    h = x1 * lax.rsqrt(ms + EPS) * n2g_ref[...]
    h2_ref[0] = (h * (1.0 + sc2_ref[0]) + sh2_ref[0]).astype(BF16)


def _channel_dft_tables():
    c, s = _dft_tables(FNET_GROUP_DIM)
    eye = np.eye(FNET_GROUPS)
    return jnp.asarray(np.kron(eye, c), F32), jnp.asarray(np.kron(eye, s), F32)


def _merge(na_o, diff_o, pr, pi, x, g1, sh2, sc2, norm2_g, subln_g, w_four, w_out, lam_init, tm):
    b, s, d = x.shape
    wc, ws = _channel_dft_tables()
    g64 = _group_mean_matrix(DIFF_WIDTH, DIFF_V_DIM)
    const = lambda shape: pl.BlockSpec(shape, lambda bi, i: (0,) * len(shape))
    per_b = pl.BlockSpec((1, 1, d), lambda bi, i: (bi, 0, 0))
    tok = lambda w: pl.BlockSpec((1, tm, w), lambda bi, i: (bi, i, 0))
    sq = (FNET_WIDTH, FNET_WIDTH)
    return pl.pallas_call(
        functools.partial(_merge_kernel, diff_scale=1.0 - lam_init),
        grid=(b, s // tm),
        in_specs=[tok(NA_WIDTH), tok(DIFF_WIDTH), tok(FNET_WIDTH), tok(FNET_WIDTH), tok(d),
                  per_b, per_b, per_b, const((1, d)), const((1, DIFF_WIDTH)),
                  const((DIFF_WIDTH, DIFF_WIDTH)), const(sq), const(sq), const(sq), const((d, d))],
        out_specs=[tok(d), tok(d)],
        out_shape=[jax.ShapeDtypeStruct((b, s, d), F32), jax.ShapeDtypeStruct((b, s, d), BF16)],
        compiler_params=_params("parallel", "parallel"),
        name="merge_projection",
    )(na_o, diff_o, pr, pi, x, g1, sh2, sc2, norm2_g, subln_g, g64, wc, ws, w_four, w_out)


FFN_TN = 256
FFN_HALO = BF16_SUBLANES


def _ffn_kernel(hp_ref, h_ref, hn_ref, wu_ref, cw_ref, cb_ref, wd_ref, x1_ref, g2_ref, o_ref,
                hcat, ug_a, uv_a, ug_b, uv_b, a_a, a_b, acc):
    i = pl.program_id(1)
    tm = h_ref.shape[1]
    nj = wd_ref.shape[0]
    hp = hp_ref[0]
    hn = hn_ref[0]
    hcat[0:FFN_HALO] = jnp.where(i > 0, hp, jnp.zeros_like(hp))
    hcat[FFN_HALO:FFN_HALO + tm] = h_ref[0]
    hcat[FFN_HALO + tm:] = jnp.where(i < pl.num_programs(1) - 1, hn, jnp.zeros_like(hn))
    acc[...] = jnp.zeros(acc.shape, F32)
    bufs = ((ug_a, uv_a), (ug_b, uv_b))
    a_bufs = (a_a, a_b)

    def project_up(j, buf):
        buf[0][...] = _dot(hcat[...], wu_ref[j])
        buf[1][...] = _dot(hcat[...], wu_ref[nj + j])

    def conv(u, cw, cb):
        total = u.shape[0]
        prv = pltpu.roll(u, 1, axis=0)[FFN_HALO:FFN_HALO + tm]
        cur = u[FFN_HALO:FFN_HALO + tm]
        nxt = pltpu.roll(u, total - 1, axis=0)[FFN_HALO:FFN_HALO + tm]
        return prv * cw[0:1] + cur * cw[1:2] + nxt * cw[2:3] + cb

    def gate(j, buf, a_buf):
        g = conv(buf[0][...], cw_ref[j], cb_ref[j])
        val = conv(buf[1][...], cw_ref[nj + j], cb_ref[nj + j])
        a_buf[...] = (g * _sigmoid(g) * val).astype(BF16)

    def step(t, par, do_up, do_gate, do_down):
        if do_up:
            project_up(t + 1, bufs[1 - par])
        if do_gate:
            gate(t, bufs[par], a_bufs[par])
        if do_down:
            acc[...] += _dot(a_bufs[1 - par][...], wd_ref[t - 1])

    project_up(0, bufs[0])
    for t in range(nj + 1):
        step(t, t % 2, t + 1 < nj, t < nj, t > 0)
    o_ref[0] = x1_ref[0] + g2_ref[0] * acc[...]


def _conv_ffn(h2, x1, g2, w_up, conv_w, conv_b, w_down, tm):
    b, s, d = x1.shape
    nj = w_down.shape[0]
    per_tile = tm // FFN_HALO
    n_halo_blocks = s // FFN_HALO
    rows = tm + 2 * FFN_HALO
    resident = lambda a: pl.BlockSpec(a.shape, lambda bi, i: (0,) * a.ndim, pipeline_mode=pl.Buffered(1))
    return pl.pallas_call(
        _ffn_kernel,
        grid=(b, s // tm),
        in_specs=[
            pl.BlockSpec((1, FFN_HALO, d), lambda bi, i: (bi, jnp.maximum(i * per_tile - 1, 0), 0)),
            pl.BlockSpec((1, tm, d), lambda bi, i: (bi, i, 0)),
            pl.BlockSpec((1, FFN_HALO, d), lambda bi, i: (bi, jnp.minimum((i + 1) * per_tile, n_halo_blocks - 1), 0)),
            resident(w_up), resident(conv_w), resident(conv_b), resident(w_down),
            pl.BlockSpec((1, tm, d), lambda bi, i: (bi, i, 0)),
            pl.BlockSpec((1, 1, d), lambda bi, i: (bi, 0, 0)),
        ],
        out_specs=pl.BlockSpec((1, tm, d), lambda bi, i: (bi, i, 0)),
        out_shape=jax.ShapeDtypeStruct((b, s, d), F32),
        scratch_shapes=[pltpu.VMEM((rows, d), BF16)] + [pltpu.VMEM((rows, FFN_TN), F32)] * 4
                       + [pltpu.VMEM((tm, FFN_TN), BF16)] * 2 + [pltpu.VMEM((tm, d), F32)],
        compiler_params=_params("parallel", "arbitrary"),
        name="conv_ffn",
    )(h2, h2, h2, w_up, conv_w, conv_b, w_down, x1, g2)


def kernel(x, c, ctx, c_ctx, w_mod, b_mod, norm1_g, w_in, na_q_g, na_k_g, na_rpb, diff_q_g, diff_k_g,
           diff_lambda, diff_subln_g, w_fourier, w_out, norm2_g, w_up, conv_w, conv_b, w_down):
    b, s, d = x.shape
    n_ctx = ctx.shape[1]
    depth = w_mod.shape[0]

    cc = jnp.zeros((8, d), F32).at[0:b].set(c).at[b].set(c_ctx)
    mod = _modulation(cc, w_mod, b_mod)

    cos_x, sin_x = _rope_tables(s)
    cos_c = jnp.ones((n_ctx, DIFF_WIDTH), F32)
    sin_c = jnp.zeros((n_ctx, DIFF_WIDTH), F32)

    cx = ctx
    for l in range(depth):
        lam_init = 0.8 - 0.6 * math.exp(-0.3 * l)
        mx = mod[l, 0:b].reshape(b, 1, 6, d)
        mc = jnp.broadcast_to(mod[l, b].reshape(1, 1, 6, d), (b, 1, 6, d))
        sh1, sc1, g1, sh2, sc2, g2 = (mx[:, :, t] for t in range(6))
        csh1, csc1, cg1, csh2, csc2, cg2 = (mc[:, :, t] for t in range(6))

        w_in_b = w_in[l].astype(BF16)
        w_out_b = w_out[l].astype(BF16)
        w_four_b = w_fourier[l].astype(BF16)
        n_up = w_up.shape[2] // FFN_TN
        w_up_b = w_up[l].astype(BF16).reshape(d, n_up, FFN_TN).transpose(1, 0, 2)
        w_down_b = w_down[l].astype(BF16).reshape(n_up // 2, FFN_TN, d)
        cw = conv_w[l].reshape(3, n_up, FFN_TN).transpose(1, 0, 2)
        n1g = norm1_g[l].reshape(1, d)
        n2g = norm2_g[l].reshape(1, d)
        gq = jnp.tile(na_q_g[l], NA_HEADS).reshape(1, NA_WIDTH)
        gk = jnp.tile(na_k_g[l], NA_HEADS).reshape(1, NA_WIDTH)
        dgq = jnp.tile(diff_q_g[l], 2 * DIFF_HEADS).reshape(1, DIFF_WIDTH)
        dgk = jnp.tile(diff_k_g[l], 2 * DIFF_HEADS).reshape(1, DIFF_WIDTH)
        subg = jnp.tile(diff_subln_g[l], DIFF_HEADS).reshape(1, DIFF_WIDTH)
        cb = conv_b[l].reshape(n_up, 1, FFN_TN)

        naq, nak, nav, dqt, dk, dvt, fu, kabs = _in_projection(
            x, sh1, sc1, n1g, w_in_b, gq, gk, dgq, dgk, cos_x, sin_x, tm=1024)
        cnaq, cnak, cnav, cdqt, cdk, cdvt, cfu, ckabs = _in_projection(
            cx, csh1, csc1, n1g, w_in_b, gq, gk, dgq, dgk, cos_c, sin_c, tm=n_ctx)
        kmax_c = jnp.max(ckabs, axis=1).reshape(b, DIFF_WIDTH, 1)
        kmax_x = jnp.maximum(jnp.max(kabs, axis=1).reshape(b, DIFF_WIDTH, 1), kmax_c)

        na_x = _neighbourhood_attention(naq, nak, nav, cnak, cnav, na_rpb[l])
        diff_x = _diff_attention(diff_lambda[l], dqt, dk, dvt, kmax_x, lam_init, tq=128,
                                 ctx_kv=(cdk, cdvt.reshape(b, DIFF_WIDTH, n_ctx)))
        pr, pi = _position_dft(fu)
        x1, h2 = _merge(na_x, diff_x, pr, pi, x, g1, sh2, sc2, n2g, subg, w_four_b, w_out_b, lam_init, tm=512)
        x = _conv_ffn(h2, x1, g2, w_up_b, cw, cb, w_down_b, tm=512)

        if l < depth - 1:
            na_c = _dense_na_attention(cnaq, cnak, cnav)
            diff_c = _diff_attention(diff_lambda[l], cdqt, cdk, cdvt, kmax_c, lam_init, tq=128)
            cpr, cpi = _position_dft_direct(cfu)
            cx1, ch2 = _merge(na_c, diff_c, cpr, cpi, cx, cg1, csh2, csc2, n2g, subg, w_four_b, w_out_b,
                              lam_init, tm=n_ctx)
            cx = _conv_ffn(ch2, cx1, cg2, w_up_b, cw, cb, w_down_b, tm=n_ctx)
    return x
```

```python
import functools
import math

import numpy as np
import jax
import jax.numpy as jnp
from jax import lax
from jax.experimental import pallas as pl
from jax.experimental.pallas import tpu as pltpu

F32 = jnp.float32
BF16 = jnp.bfloat16

GRID_W = 64
EPS = 1e-6
ROPE_BASE = 10000.0
HEAD_DIM = 64
NA_HEADS = 8
NA_WIDTH = NA_HEADS * HEAD_DIM
NA_KH = 8
NA_KW = 16
DIFF_HEADS = 4
DIFF_QK_DIM = 32
DIFF_V_DIM = 64
DIFF_WIDTH = DIFF_HEADS * DIFF_V_DIM
FNET_GROUPS = 4
FNET_GROUP_DIM = 64
FNET_WIDTH = FNET_GROUPS * FNET_GROUP_DIM
NA_Q0 = 0
NA_K0 = NA_Q0 + NA_WIDTH
NA_V0 = NA_K0 + NA_WIDTH
DQ0 = NA_V0 + NA_WIDTH
DK0 = DQ0 + DIFF_WIDTH
DV0 = DK0 + DIFF_WIDTH
FN0 = DV0 + DIFF_WIDTH
IN_WIDTH = FN0 + FNET_WIDTH
NEG_BIG = -1e30
LOG2E = 1.4426950408889634

VMEM_LIMIT_BYTES = 52 * 1024 * 1024
LANES = 128
BF16_SUBLANES = 16


def _params(*sem, flags=None):
    return pltpu.CompilerParams(dimension_semantics=sem, vmem_limit_bytes=VMEM_LIMIT_BYTES, flags=flags)


def _dot(a, b):
    return jnp.dot(a, b, preferred_element_type=F32)


def _dot_nt(a, b):
    return lax.dot_general(a, b, (((1,), (1,)), ((), ())), preferred_element_type=F32)


def _sigmoid(x):
    return 1.0 / (1.0 + jnp.exp(-x))


def _group_mean_matrix(width, group):
    idx = np.arange(width) // group
    return jnp.asarray((idx[:, None] == idx[None, :]).astype(np.float32) / group, dtype=BF16)


def _mod_kernel(c_ref, w_ref, b_ref, o_ref):
    c = c_ref[...]
    s = c * _sigmoid(c)
    s_hi = s.astype(BF16)
    s_lo = (s - s_hi.astype(F32)).astype(BF16)
    w = w_ref[0]
    w_hi = w.astype(BF16)
    w_lo = (w - w_hi.astype(F32)).astype(BF16)
    o_ref[0] = _dot(s_hi, w_hi) + _dot(s_hi, w_lo) + _dot(s_lo, w_hi) + b_ref[0]


def _modulation(cc, w_mod, b_mod):
    depth, d, n = w_mod.shape
    tn = 768
    return pl.pallas_call(
        _mod_kernel,
        grid=(depth, n // tn),
        in_specs=[
            pl.BlockSpec((8, d), lambda l, j: (0, 0)),
            pl.BlockSpec((1, d, tn), lambda l, j: (l, 0, j)),
            pl.BlockSpec((1, 1, tn), lambda l, j: (l, 0, j)),
        ],
        out_specs=pl.BlockSpec((1, 8, tn), lambda l, j: (l, 0, j)),
        out_shape=jax.ShapeDtypeStruct((depth, 8, n), F32),
        compiler_params=_params("parallel", "parallel"),
        name="modulation",
    )(cc, w_mod, b_mod.reshape(depth, 1, n))


def _inproj_kernel(x_ref, sh_ref, sc_ref, g_ref, w_ref, gq_ref, gk_ref, dgq_ref, dgk_ref,
                   g64_ref, g32_ref, cos_ref, sin_ref,
                   naq_ref, nak_ref, nav_ref, dqt_ref, dk_ref, dvt_ref, fu_ref, kabs_ref):
    x = x_ref[0]
    ms = jnp.mean(x * x, axis=-1, keepdims=True)
    h = x * lax.rsqrt(ms + EPS) * g_ref[...]
    h = h * (1.0 + sc_ref[0]) + sh_ref[0]
    hb = h.astype(BF16)

    def proj(c0, c1):
        return _dot(hb, w_ref[:, c0:c1])

    def group_norm(p, gmat_ref, gain):
        gms = _dot((p * p).astype(BF16), gmat_ref[...])
        return p * lax.rsqrt(gms + EPS) * gain

    def rope(y):
        lane = lax.broadcasted_iota(jnp.int32, y.shape, 1)
        nxt = pltpu.roll(y, DIFF_WIDTH - 8, axis=1)
        prv = pltpu.roll(y, 8, axis=1)
        partner = jnp.where((lane & 8) == 0, nxt, prv)
        return y * cos_ref[...] + partner * sin_ref[...]

    q = group_norm(proj(NA_Q0, NA_K0), g64_ref, gq_ref[...]) * (HEAD_DIM ** -0.5)
    naq_ref[0] = q.astype(BF16)
    k = group_norm(proj(NA_K0, NA_V0), g64_ref, gk_ref[...])
    nak_ref[0] = k.astype(BF16)
    nav_ref[0] = proj(NA_V0, DQ0).astype(BF16)
    dq = rope(group_norm(proj(DQ0, DK0), g32_ref, dgq_ref[...])) * (DIFF_QK_DIM ** -0.5 * LOG2E)
    dqt_ref[0] = dq.T.astype(BF16)
    dk = rope(group_norm(proj(DK0, DV0), g32_ref, dgk_ref[...]))
    dkb = dk.astype(BF16)
    dk_ref[0] = dkb
    kabs_ref[0, 0] = jnp.max(jnp.abs(dkb.astype(F32)), axis=0, keepdims=True)
    dvt_ref[0, 0] = proj(DV0, FN0).T.astype(BF16)
    fu_ref[0] = proj(FN0, IN_WIDTH).astype(BF16)


def _in_projection(x, shift, scale, norm_g, w_in, gq, gk, dgq, dgk, cos_t, sin_t, tm):
    b, s, d = x.shape
    g64 = _group_mean_matrix(NA_WIDTH, HEAD_DIM)
    g32 = _group_mean_matrix(DIFF_WIDTH, DIFF_QK_DIM)
    const = lambda shape: pl.BlockSpec(shape, lambda bi, i: (0,) * len(shape))
    per_b = pl.BlockSpec((1, 1, d), lambda bi, i: (bi, 0, 0))
    tok = lambda w: pl.BlockSpec((1, tm, w), lambda bi, i: (bi, i, 0))
    tok_t = lambda w: pl.BlockSpec((1, w, tm), lambda bi, i: (bi, 0, i))
    sds = jax.ShapeDtypeStruct
    return pl.pallas_call(
        _inproj_kernel,
        grid=(b, s // tm),
        in_specs=[
            tok(d), per_b, per_b, const((1, d)), const((d, IN_WIDTH)),
            const((1, NA_WIDTH)), const((1, NA_WIDTH)), const((1, DIFF_WIDTH)), const((1, DIFF_WIDTH)),
            const((NA_WIDTH, NA_WIDTH)), const((DIFF_WIDTH, DIFF_WIDTH)),
            pl.BlockSpec((tm, DIFF_WIDTH), lambda bi, i: (i, 0)),
            pl.BlockSpec((tm, DIFF_WIDTH), lambda bi, i: (i, 0)),
        ],
        out_specs=[tok(NA_WIDTH), tok(NA_WIDTH), tok(NA_WIDTH),
                   tok_t(DIFF_WIDTH), tok(DIFF_WIDTH),
                   pl.BlockSpec((1, 1, DIFF_WIDTH, tm), lambda bi, i: (bi, i, 0, 0)), tok(FNET_WIDTH),
                   pl.BlockSpec((1, 1, 1, DIFF_WIDTH), lambda bi, i: (bi, i, 0, 0))],
        out_shape=[sds((b, s, NA_WIDTH), BF16), sds((b, s, NA_WIDTH), BF16), sds((b, s, NA_WIDTH), BF16),
                   sds((b, DIFF_WIDTH, s), BF16), sds((b, s, DIFF_WIDTH), BF16),
                   sds((b, s // tm, DIFF_WIDTH, tm), BF16), sds((b, s, FNET_WIDTH), BF16),
                   sds((b, s // tm, 1, DIFF_WIDTH), F32)],
        compiler_params=_params("parallel", "parallel"),
        name="in_projection",
    )(x, shift, scale, norm_g, w_in, gq, gk, dgq, dgk, g64, g32, cos_t, sin_t)


def _rope_tables(s):
    pos = jnp.arange(s)
    rows = (pos // GRID_W).astype(F32)
    cols = (pos % GRID_W).astype(F32)
    m = DIFF_QK_DIM // 2
    inv = ROPE_BASE ** (-jnp.arange(0, m, 2, dtype=F32) / m)
    ang_r = rows[:, None] * inv[None, :]
    ang_c = cols[:, None] * inv[None, :]
    ang = jnp.concatenate([ang_r, ang_r, ang_c, ang_c], axis=1)
    sign = jnp.asarray(np.tile(np.repeat([-1.0, 1.0], 8), 2), F32)
    cos32 = jnp.cos(ang)
    sin32 = jnp.sin(ang) * sign[None, :]
    reps = DIFF_WIDTH // DIFF_QK_DIM
    return jnp.tile(cos32, (1, reps)), jnp.tile(sin32, (1, reps))


NA_GROUP_ROWS = 8
NA_HALF = NA_WIDTH // 2
NA_HEADS_PER_HALF = NA_HALF // HEAD_DIM


def _head_block_mask(rows_per_head, n_heads, width):
    r = lax.broadcasted_iota(jnp.int32, (rows_per_head * n_heads, width), 0) // rows_per_head
    c = lax.broadcasted_iota(jnp.int32, (rows_per_head * n_heads, width), 1) // (width // n_heads)
    return r == c


def _na_kernel(q_ref, kp_ref, k0_ref, kn_ref, vp_ref, v0_ref, vn_ref, kc_ref, vc_ref, bias_ref,
               o_ref, kbuf, vbuf, s_a, s_b):
    g = pl.program_id(1)
    blk = NA_GROUP_ROWS * GRID_W
    kbuf[0:blk] = kp_ref[0]
    kbuf[blk:2 * blk] = k0_ref[0]
    kbuf[2 * blk:3 * blk] = kn_ref[0]
    vbuf[0:blk] = vp_ref[0]
    vbuf[blk:2 * blk] = v0_ref[0]
    vbuf[2 * blk:3 * blk] = vn_ref[0]
    n_rows = pl.num_programs(1) * NA_GROUP_ROWS
    head_mask = _head_block_mask(GRID_W, NA_HEADS_PER_HALF, NA_HALF)
    n_keys = NA_KH * GRID_W

    s_bufs = (s_a, s_b)

    def window(i):
        r = g * NA_GROUP_ROWS + i
        row_start = jnp.clip(r - NA_KH // 2, 0, n_rows - NA_KH)
        off = pl.multiple_of((row_start - g * NA_GROUP_ROWS + NA_GROUP_ROWS) * GRID_W, GRID_W)
        return off, row_start - r + (NA_KH - 1)

    def scores(t, s_buf):
        i, hh = divmod(t, 2)
        off, d0 = window(i)
        cs = slice(hh * NA_HALF, (hh + 1) * NA_HALF)
        qh = q_ref[0, i * GRID_W:(i + 1) * GRID_W, cs]
        qst = jnp.concatenate([qh] * NA_HEADS_PER_HALF, axis=0)
        qst = jnp.where(head_mask, qst, jnp.zeros_like(qst))
        bias = jnp.concatenate([bias_ref[d0 + 2 * j, hh] for j in range(NA_KH // 2)], axis=1)
        s_buf[:, 0:n_keys] = _dot_nt(qst, kbuf[pl.ds(off, n_keys), cs]) + bias
        s_buf[:, n_keys:] = _dot_nt(qst, kc_ref[0, :, cs])

    def finish(t, s_buf):
        i, hh = divmod(t, 2)
        off, _ = window(i)
        cs = slice(hh * NA_HALF, (hh + 1) * NA_HALF)
        s = s_buf[...]
        m = jnp.max(s, axis=-1, keepdims=True)
        e = jnp.exp(s - m)
        l = jnp.sum(e, axis=-1, keepdims=True)
        eb = e.astype(BF16)
        o = _dot(eb[:, 0:n_keys], vbuf[pl.ds(off, n_keys), cs]) + _dot(eb[:, n_keys:], vc_ref[0, :, cs])
        o = jnp.where(head_mask, o * (1.0 / l), 0.0)
        out = o[0:GRID_W]
        for hl in range(1, NA_HEADS_PER_HALF):
            out = out + o[hl * GRID_W:(hl + 1) * GRID_W]
        o_ref[0, i * GRID_W:(i + 1) * GRID_W, cs] = out.astype(o_ref.dtype)

    n_jobs = 2 * NA_GROUP_ROWS
    scores(0, s_a)
    for t in range(n_jobs):
        if t + 1 < n_jobs:
            scores(t + 1, s_bufs[(t + 1) % 2])
        finish(t, s_bufs[t % 2])


def _na_bias_table(rpb):
    h, n_dr, _ = rpb.shape
    qc = np.arange(GRID_W)
    kc = np.arange(GRID_W)
    col_start = np.clip(qc - NA_KW // 2, 0, GRID_W - NA_KW)
    valid = (kc[None, :] >= col_start[:, None]) & (kc[None, :] < col_start[:, None] + NA_KW)
    pad = GRID_W - NA_KW
    period = 2 * GRID_W - 1
    rext = jnp.pad(rpb, ((0, 0), (0, 0), (pad, pad)))
    flat = jnp.tile(rext, (1, 1, GRID_W + 1))[:, :, :GRID_W * (period + 1)]
    hankel = flat.reshape(h, n_dr, GRID_W, period + 1)[..., :GRID_W]
    toep = hankel[:, :, ::-1, :]
    toep = jnp.where(jnp.asarray(valid)[None, None], toep, NEG_BIG)
    pair = jnp.concatenate([toep[:, :-1], toep[:, 1:]], axis=-1)
    pair = pair.reshape(2, NA_HEADS_PER_HALF, n_dr - 1, GRID_W, 2 * GRID_W).transpose(2, 0, 1, 3, 4)
    return pair.reshape(n_dr - 1, 2, NA_HEADS_PER_HALF * GRID_W, 2 * GRID_W).astype(F32)


def _neighbourhood_attention(q, k, v, kc, vc, rpb):
    b, s, _ = q.shape
    n_groups = s // (NA_GROUP_ROWS * GRID_W)
    blk = NA_GROUP_ROWS * GRID_W
    c = kc.shape[1]
    bias = _na_bias_table(rpb)
    cur = pl.BlockSpec((1, blk, NA_WIDTH), lambda bi, g: (bi, g, 0))
    prv = pl.BlockSpec((1, blk, NA_WIDTH), lambda bi, g: (bi, jnp.maximum(g - 1, 0), 0))
    nxt = pl.BlockSpec((1, blk, NA_WIDTH), lambda bi, g: (bi, jnp.minimum(g + 1, n_groups - 1), 0))
    ctx = pl.BlockSpec((1, c, NA_WIDTH), lambda bi, g: (bi, 0, 0))
    return pl.pallas_call(
        _na_kernel,
        grid=(b, n_groups),
        in_specs=[cur, prv, cur, nxt, prv, cur, nxt, ctx, ctx,
                  pl.BlockSpec(bias.shape, lambda bi, g: (0, 0, 0, 0))],
        out_specs=cur,
        out_shape=jax.ShapeDtypeStruct((b, s, NA_WIDTH), BF16),
        scratch_shapes=[pltpu.VMEM((3 * blk, NA_WIDTH), BF16), pltpu.VMEM((3 * blk, NA_WIDTH), BF16)]
                       + [pltpu.VMEM((NA_HEADS_PER_HALF * GRID_W, NA_KH * GRID_W + c), F32)] * 2,
        compiler_params=_params("parallel", "arbitrary"),
        name="neighbourhood_attention",
    )(q, k, k, k, v, v, v, kc, vc, bias)


def _dense_na_kernel(q_ref, k_ref, v_ref, o_ref):
    c = q_ref.shape[1]
    head_mask = _head_block_mask(c, NA_HEADS_PER_HALF, NA_HALF)
    for hh in range(2):
        cs = slice(hh * NA_HALF, (hh + 1) * NA_HALF)
        qh = q_ref[0, :, cs]
        qst = jnp.concatenate([qh] * NA_HEADS_PER_HALF, axis=0)
        qst = jnp.where(head_mask, qst, jnp.zeros_like(qst))
        s = _dot_nt(qst, k_ref[0, :, cs])
        m = jnp.max(s, axis=-1, keepdims=True)
        e = jnp.exp(s - m)
        l = jnp.sum(e, axis=-1, keepdims=True)
        o = _dot(e.astype(BF16), v_ref[0, :, cs])
        o = jnp.where(head_mask, o * (1.0 / l), 0.0)
        out = o[0:c]
        for hl in range(1, NA_HEADS_PER_HALF):
            out = out + o[hl * c:(hl + 1) * c]
        o_ref[0, :, cs] = out.astype(o_ref.dtype)


def _dense_na_attention(q, k, v):
    b, c, w = q.shape
    spec = pl.BlockSpec((1, c, w), lambda bi: (bi, 0, 0))
    return pl.pallas_call(
        _dense_na_kernel,
        grid=(b,),
        in_specs=[spec, spec, spec],
        out_specs=spec,
        out_shape=jax.ShapeDtypeStruct((b, c, w), BF16),
        compiler_params=_params("parallel"),
        name="context_dense_attention",
    )(q, k, v)


N_DIFF_STREAMS = 2 * DIFF_HEADS


DIFF_HEAD_PAIRS = DIFF_HEADS // 2
DIFF_MIN_TRUSTED_SUM = 2.0 ** -80


def _diff_kernel(*refs, lam_init, has_ctx):
    n_in = 7 if has_ctx else 5
    lam_ref, qt_ref, k_ref, vt_ref, kmax_ref = refs[:5]
    kc_ref, vct_ref = refs[5:7] if has_ctx else (None, None)
    o_ref, qs, s_a, s_b, p_a, p_b, m_scr, l_scr, acc = refs[n_in:]
    tq = qt_ref.shape[2]
    n_chunks, _, tk = vt_ref.shape[1:]
    pair_w = 4 * tq

    qt = qt_ref[0]
    stream = lax.broadcasted_iota(jnp.int32, qt.shape, 0) // DIFF_QK_DIM
    for st in range(N_DIFF_STREAMS):
        qs[:, st * tq:(st + 1) * tq] = jnp.where(stream == st, qt, jnp.zeros_like(qt))
    bound = jnp.sum(jnp.abs(qs[...].astype(F32)) * kmax_ref[0], axis=0, keepdims=True)
    m_scr[...] = bound * (1.0 + 2.0 ** -10) + 2.0 ** -10
    l_scr[...] = jnp.zeros(l_scr.shape, F32)
    acc[...] = jnp.zeros(acc.shape, F32)

    def scores(kk, s_buf):
        n = kk.shape[0]
        for pair in range(DIFF_HEAD_PAIRS):
            cols = slice(pair * pair_w, (pair + 1) * pair_w)
            s_buf[0:n, cols] = _dot(kk, qs[:, cols])

    def probs_bounded(kk, p_buf):
        n = kk.shape[0]
        for pair in range(DIFF_HEAD_PAIRS):
            cols = slice(pair * pair_w, (pair + 1) * pair_w)
            p = jnp.exp2(_dot(kk, qs[:, cols]) - m_scr[:, cols])
            l_scr[:, cols] += jnp.sum(p, axis=0, keepdims=True)
            p_buf[0:n, cols] = p.astype(BF16)

    def values_bounded(p_buf, n, vt_rows):
        for h in range(DIFF_HEADS):
            rows = slice(h * DIFF_V_DIM, (h + 1) * DIFF_V_DIM)
            cols = slice(2 * h * tq, (2 * h + 2) * tq)
            acc[rows, :] += _dot(vt_rows(rows), p_buf[0:n, cols])

    def absorb(s_buf, n, vt_rows):
        for pair in range(DIFF_HEAD_PAIRS):
            cols = slice(pair * pair_w, (pair + 1) * pair_w)
            s = s_buf[0:n, cols]
            m_old = m_scr[:, cols]
            m_new = jnp.maximum(m_old, jnp.max(s, axis=0, keepdims=True))
            alpha = jnp.exp2(m_old - m_new)
            p = jnp.exp2(s - m_new)
            l_scr[:, cols] = alpha * l_scr[:, cols] + jnp.sum(p, axis=0, keepdims=True)
            m_scr[:, cols] = m_new
            pb = p.astype(BF16)
            for hh in range(2):
                h = 2 * pair + hh
                rows = slice(h * DIFF_V_DIM, (h + 1) * DIFF_V_DIM)
                hc = slice(hh * 2 * tq, (hh + 1) * 2 * tq)
                acc[rows, :] = acc[rows, :] * alpha[:, hc] + _dot(vt_rows(rows), pb[:, hc])

    n_items = n_chunks + (1 if has_ctx else 0)

    def all_items(scores_fn, absorb_fn, bufs):
        def scores_item(t, buf):
            if has_ctx and isinstance(t, int) and t == n_chunks:
                scores_fn(kc_ref[0], buf)
            else:
                start = t * tk if isinstance(t, int) else pl.multiple_of(t * tk, tk)
                scores_fn(k_ref[0, pl.ds(start, tk), :], buf)

        def absorb_item(t, buf):
            if has_ctx and isinstance(t, int) and t == n_chunks:
                absorb_fn(buf, kc_ref.shape[1], lambda rows: vct_ref[0, rows, :])
            else:
                absorb_fn(buf, tk, lambda rows: vt_ref[0, t, rows, :])

        scores_item(0, bufs[0])
        n_pairs = (n_chunks - 1) // 2

        def body(i, carry):
            t = 2 * i
            scores_item(t + 1, bufs[1])
            absorb_item(t, bufs[0])
            scores_item(t + 2, bufs[0])
            absorb_item(t + 1, bufs[1])
            return carry

        lax.fori_loop(0, n_pairs, body, 0)
        for t in range(2 * n_pairs, n_items):
            if t + 1 < n_items:
                scores_item(t + 1, bufs[(t + 1) % 2])
            absorb_item(t, bufs[t % 2])

    all_items(probs_bounded, values_bounded, (p_a, p_b))

    @pl.when(jnp.min(l_scr[...]) < DIFF_MIN_TRUSTED_SUM)
    def _():
        m_scr[...] = jnp.full(m_scr.shape, NEG_BIG, F32)
        l_scr[...] = jnp.zeros(l_scr.shape, F32)
        acc[...] = jnp.zeros(acc.shape, F32)
        all_items(scores, absorb, (s_a, s_b))

    lp = lam_ref[...]
    lam = (jnp.exp(jnp.sum(lp[0:1] * lp[1:2], axis=1, keepdims=True))
           - jnp.exp(jnp.sum(lp[2:3] * lp[3:4], axis=1, keepdims=True)) + lam_init)
    inv_l = 1.0 / l_scr[...]
    outs = []
    for h in range(DIFF_HEADS):
        rows = slice(h * DIFF_V_DIM, (h + 1) * DIFF_V_DIM)
        o1 = acc[rows, 0:tq] * inv_l[:, 2 * h * tq:(2 * h + 1) * tq]
        o2 = acc[rows, tq:2 * tq] * inv_l[:, (2 * h + 1) * tq:(2 * h + 2) * tq]
        outs.append(o1 - lam * o2)
    o_ref[0] = jnp.concatenate(outs, axis=0).T


def _diff_attention(lam_p, qt, k, vt, kmax, lam_init, tq, ctx_kv=None):
    b, w, sq = qt.shape
    n_chunks, _, tk = vt.shape[1:]
    whole = lambda a: pl.BlockSpec((1,) + a.shape[1:], lambda bi, i: (bi,) + (0,) * (a.ndim - 1))
    operands = [lam_p, qt, k, vt, kmax]
    in_specs = [
        pl.BlockSpec(lam_p.shape, lambda bi, i: (0, 0)),
        pl.BlockSpec((1, w, tq), lambda bi, i: (bi, 0, i)),
        whole(k), whole(vt), whole(kmax),
    ]
    if ctx_kv is not None:
        operands += list(ctx_kv)
        in_specs += [whole(ctx_kv[0]), whole(ctx_kv[1])]
    return pl.pallas_call(
        functools.partial(_diff_kernel, lam_init=lam_init, has_ctx=ctx_kv is not None),
        grid=(b, sq // tq),
        in_specs=in_specs,
        out_specs=pl.BlockSpec((1, tq, w), lambda bi, i: (bi, i, 0)),
        out_shape=jax.ShapeDtypeStruct((b, sq, w), F32),
        scratch_shapes=[
            pltpu.VMEM((w, N_DIFF_STREAMS * tq), BF16),
            pltpu.VMEM((tk, N_DIFF_STREAMS * tq), F32),
            pltpu.VMEM((tk, N_DIFF_STREAMS * tq), F32),
            pltpu.VMEM((tk, N_DIFF_STREAMS * tq), BF16),
            pltpu.VMEM((tk, N_DIFF_STREAMS * tq), BF16),
            pltpu.VMEM((1, N_DIFF_STREAMS * tq), F32),
            pltpu.VMEM((1, N_DIFF_STREAMS * tq), F32),
            pltpu.VMEM((w, 2 * tq), F32),
        ],
        compiler_params=_params("parallel", "arbitrary"),
        name="differential_attention",
    )(*operands)


FFT_N1 = 64
FFT_N1_STEP = 8


def _dft_tables(n):
    idx = np.arange(n)
    ang = 2.0 * np.pi * ((idx[:, None] * idx[None, :]) % n) / n
    return np.cos(ang), np.sin(ang)


def _fft_stage_a_kernel(x_ref, t_ref, twc_ref, tws_ref, ur_ref, ui_ref):
    n2 = x_ref.shape[1]
    t = t_ref[...].astype(BF16)
    for i in range(FFT_N1_STEP):
        xs = x_ref[0, :, i * FNET_WIDTH:(i + 1) * FNET_WIDTH]
        u = _dot(t, xs)
        ur, ui = u[0:n2], u[n2:2 * n2]
        c = jnp.concatenate([twc_ref[i]] * (FNET_WIDTH // LANES), axis=1)
        s = jnp.concatenate([tws_ref[i]] * (FNET_WIDTH // LANES), axis=1)
        ur_ref[0, i] = (ur * c + ui * s).astype(BF16)
        ui_ref[0, i] = (ui * c - ur * s).astype(BF16)


def _fft_stage_c_kernel(ur_ref, ui_ref, c_ref, s_ref, pr_ref, pi_ref):
    ur = ur_ref[0]
    ui = ui_ref[0]
    c1 = c_ref[...].astype(BF16)
    s1 = s_ref[...].astype(BF16)
    pr_ref[0] = (_dot(c1, ur) + _dot(s1, ui)).astype(BF16)
    pi_ref[0] = (_dot(c1, ui) - _dot(s1, ur)).astype(BF16)


def _position_dft(fu):
    b, l, w = fu.shape
    n1, n2 = FFT_N1, l // FFT_N1
    scale = 1.0 / math.sqrt(l * FNET_GROUP_DIM)
    c2, s2 = _dft_tables(n2)
    t2 = jnp.asarray(np.concatenate([c2, -s2], axis=0) * scale, dtype=F32)
    ang = (2.0 * jnp.pi / l) * (jnp.arange(n1, dtype=F32)[:, None] * jnp.arange(n2, dtype=F32)[None, :])
    twc = jnp.broadcast_to(jnp.cos(ang)[:, :, None], (n1, n2, LANES))
    tws = jnp.broadcast_to(jnp.sin(ang)[:, :, None], (n1, n2, LANES))
    xv = fu.reshape(b, n2, n1 * w)
    sds = jax.ShapeDtypeStruct
    ur, ui = pl.pallas_call(
        _fft_stage_a_kernel,
        grid=(b, n1 // FFT_N1_STEP),
        in_specs=[
            pl.BlockSpec((1, n2, FFT_N1_STEP * w), lambda bi, j: (bi, 0, j)),
            pl.BlockSpec((2 * n2, n2), lambda bi, j: (0, 0)),
            pl.BlockSpec((FFT_N1_STEP, n2, LANES), lambda bi, j: (j, 0, 0)),
            pl.BlockSpec((FFT_N1_STEP, n2, LANES), lambda bi, j: (j, 0, 0)),
        ],
        out_specs=[pl.BlockSpec((1, FFT_N1_STEP, n2, w), lambda bi, j: (bi, j, 0, 0))] * 2,
        out_shape=[sds((b, n1, n2, w), BF16)] * 2,
        compiler_params=_params("parallel", "parallel"),
        name="position_dft_stage_a",
    )(xv, t2, twc, tws)
    c1, s1 = _dft_tables(n1)
    tn = 4096
    cols = n2 * w
    blk = pl.BlockSpec((1, n1, tn), lambda bi, j: (bi, 0, j))
    tab = pl.BlockSpec((n1, n1), lambda bi, j: (0, 0))
    pr, pi = pl.pallas_call(
        _fft_stage_c_kernel,
        grid=(b, cols // tn),
        in_specs=[blk, blk, tab, tab],
        out_specs=[blk, blk],
        out_shape=[sds((b, n1, cols), BF16)] * 2,
        compiler_params=_params("parallel", "parallel"),
        name="position_dft_stage_c",
    )(ur.reshape(b, n1, cols), ui.reshape(b, n1, cols), jnp.asarray(c1, F32), jnp.asarray(s1, F32))
    return pr.reshape(b, l, w), pi.reshape(b, l, w)


def _direct_dft_kernel(x_ref, t_ref, pr_ref, pi_ref):
    l = x_ref.shape[1]
    u = _dot(t_ref[...].astype(BF16), x_ref[0])
    pr_ref[0] = u[0:l].astype(BF16)
    pi_ref[0] = u[l:2 * l].astype(BF16)


def _position_dft_direct(fu):
    b, l, w = fu.shape
    scale = 1.0 / math.sqrt(l * FNET_GROUP_DIM)
    c, s = _dft_tables(l)
    t = jnp.asarray(np.concatenate([c, -s], axis=0) * scale, dtype=F32)
    spec = pl.BlockSpec((1, l, w), lambda bi: (bi, 0, 0))
    return pl.pallas_call(
        _direct_dft_kernel,
        grid=(b,),
        in_specs=[spec, pl.BlockSpec((2 * l, l), lambda bi: (0, 0))],
        out_specs=[spec, spec],
        out_shape=[jax.ShapeDtypeStruct((b, l, w), BF16)] * 2,
        compiler_params=_params("parallel"),
        name="context_position_dft",
    )(fu, t)


def _merge_kernel(na_ref, df_ref, pr_ref, pi_ref, x_ref, g1_ref, sh2_ref, sc2_ref, n2g_ref, subg_ref,
                  g64_ref, wc_ref, ws_ref, wf_ref, wo_ref, x1_ref, h2_ref, *, diff_scale):
    d = df_ref[0]
    gms = _dot((d * d).astype(BF16), g64_ref[...])
    dn = d * lax.rsqrt(gms + EPS) * subg_ref[...] * diff_scale
    fr = _dot(pr_ref[0], wc_ref[...].astype(BF16)) + _dot(pi_ref[0], ws_ref[...].astype(BF16))
    four = _dot(fr.astype(BF16), wf_ref[...])
    y = (_dot(na_ref[0], wo_ref[0:NA_WIDTH, :])
         + _dot(dn.astype(BF16), wo_ref[NA_WIDTH:NA_WIDTH + DIFF_WIDTH, :])
         + _dot(four.astype(BF16), wo_ref[NA_WIDTH + DIFF_WIDTH:, :]))
    x1 = x_ref[0] + g1_ref[0] * y
    x1_ref[0] = x1
    ms = jnp.mean(x1 * x1, axis=-1, keepdims=True)
    h = x1 * lax.rsqrt(ms + EPS) * n2g_ref[...]
    h2_ref[0] = (h * (1.0 + sc2_ref[0]) + sh2_ref[0]).astype(BF16)


def _channel_dft_tables():
    c, s = _dft_tables(FNET_GROUP_DIM)
    eye = np.eye(FNET_GROUPS)
    return jnp.asarray(np.kron(eye, c), F32), jnp.asarray(np.kron(eye, s), F32)


def _merge(na_o, diff_o, pr, pi, x, g1, sh2, sc2, norm2_g, subln_g, w_four, w_out, lam_init, tm):
    b, s, d = x.shape
    wc, ws = _channel_dft_tables()
    g64 = _group_mean_matrix(DIFF_WIDTH, DIFF_V_DIM)
    const = lambda shape: pl.BlockSpec(shape, lambda bi, i: (0,) * len(shape))
    per_b = pl.BlockSpec((1, 1, d), lambda bi, i: (bi, 0, 0))
    tok = lambda w: pl.BlockSpec((1, tm, w), lambda bi, i: (bi, i, 0))
    sq = (FNET_WIDTH, FNET_WIDTH)
    return pl.pallas_call(
        functools.partial(_merge_kernel, diff_scale=1.0 - lam_init),
        grid=(b, s // tm),
        in_specs=[tok(NA_WIDTH), tok(DIFF_WIDTH), tok(FNET_WIDTH), tok(FNET_WIDTH), tok(d),
                  per_b, per_b, per_b, const((1, d)), const((1, DIFF_WIDTH)),
                  const((DIFF_WIDTH, DIFF_WIDTH)), const(sq), const(sq), const(sq), const((d, d))],
        out_specs=[tok(d), tok(d)],
        out_shape=[jax.ShapeDtypeStruct((b, s, d), F32), jax.ShapeDtypeStruct((b, s, d), BF16)],
        compiler_params=_params("parallel", "parallel"),
        name="merge_projection",
    )(na_o, diff_o, pr, pi, x, g1, sh2, sc2, norm2_g, subln_g, g64, wc, ws, w_four, w_out)


FFN_TN = 256
FFN_HALO = BF16_SUBLANES


def _ffn_kernel(hp_ref, h_ref, hn_ref, wu_ref, cw_ref, cb_ref, wd_ref, x1_ref, g2_ref, o_ref,
                hcat, ug_a, uv_a, ug_b, uv_b, a_a, a_b, acc):
    i = pl.program_id(1)
    tm = h_ref.shape[1]
    nj = wd_ref.shape[0]
    hp = hp_ref[0]
    hn = hn_ref[0]
    hcat[0:FFN_HALO] = jnp.where(i > 0, hp, jnp.zeros_like(hp))
    hcat[FFN_HALO:FFN_HALO + tm] = h_ref[0]
    hcat[FFN_HALO + tm:] = jnp.where(i < pl.num_programs(1) - 1, hn, jnp.zeros_like(hn))
    acc[...] = jnp.zeros(acc.shape, F32)
    bufs = ((ug_a, uv_a), (ug_b, uv_b))
    a_bufs = (a_a, a_b)

    def project_up(j, buf):
        gc = slice(j * FFN_TN, (j + 1) * FFN_TN)
        vc = slice((nj + j) * FFN_TN, (nj + j + 1) * FFN_TN)
        buf[0][...] = _dot(hcat[...], wu_ref[:, gc])
        buf[1][...] = _dot(hcat[...], wu_ref[:, vc])

    def conv(u, cw, cb):
        total = u.shape[0]
        prv = pltpu.roll(u, 1, axis=0)[FFN_HALO:FFN_HALO + tm]
        cur = u[FFN_HALO:FFN_HALO + tm]
        nxt = pltpu.roll(u, total - 1, axis=0)[FFN_HALO:FFN_HALO + tm]
        return prv * cw[0:1] + cur * cw[1:2] + nxt * cw[2:3] + cb

    def gate(j, buf, a_buf):
        g = conv(buf[0][...], cw_ref[j], cb_ref[j])
        val = conv(buf[1][...], cw_ref[nj + j], cb_ref[nj + j])
        a_buf[...] = (g * _sigmoid(g) * val).astype(BF16)

    def step(t, par, do_up, do_gate, do_down):
        if do_up:
            project_up(t + 1, bufs[1 - par])
        if do_gate:
            gate(t, bufs[par], a_bufs[par])
        if do_down:
            acc[...] += _dot(a_bufs[1 - par][...], wd_ref[t - 1])

    project_up(0, bufs[0])
    for t in range(nj + 1):
        step(t, t % 2, t + 1 < nj, t < nj, t > 0)
    o_ref[0] = x1_ref[0] + g2_ref[0] * acc[...]


def _conv_ffn(h2, x1, g2, w_up, conv_w, conv_b, w_down, tm):
    b, s, d = x1.shape
    nj = w_down.shape[0]
    per_tile = tm // FFN_HALO
    n_halo_blocks = s // FFN_HALO
    rows = tm + 2 * FFN_HALO
    resident = lambda a: pl.BlockSpec(a.shape, lambda bi, i: (0,) * a.ndim, pipeline_mode=pl.Buffered(1))
    return pl.pallas_call(
        _ffn_kernel,
        grid=(b, s // tm),
        in_specs=[
            pl.BlockSpec((1, FFN_HALO, d), lambda bi, i: (bi, jnp.maximum(i * per_tile - 1, 0), 0)),
            pl.BlockSpec((1, tm, d), lambda bi, i: (bi, i, 0)),
            pl.BlockSpec((1, FFN_HALO, d), lambda bi, i: (bi, jnp.minimum((i + 1) * per_tile, n_halo_blocks - 1), 0)),
            resident(w_up), resident(conv_w), resident(conv_b), resident(w_down),
            pl.BlockSpec((1, tm, d), lambda bi, i: (bi, i, 0)),
            pl.BlockSpec((1, 1, d), lambda bi, i: (bi, 0, 0)),
        ],
        out_specs=pl.BlockSpec((1, tm, d), lambda bi, i: (bi, i, 0)),
        out_shape=jax.ShapeDtypeStruct((b, s, d), F32),
        scratch_shapes=[pltpu.VMEM((rows, d), BF16)] + [pltpu.VMEM((rows, FFN_TN), F32)] * 4
                       + [pltpu.VMEM((tm, FFN_TN), BF16)] * 2 + [pltpu.VMEM((tm, d), F32)],
        compiler_params=_params("parallel", "arbitrary"),
        name="conv_ffn",
    )(h2, h2, h2, w_up, conv_w, conv_b, w_down, x1, g2)


def kernel(x, c, ctx, c_ctx, w_mod, b_mod, norm1_g, w_in, na_q_g, na_k_g, na_rpb, diff_q_g, diff_k_g,
           diff_lambda, diff_subln_g, w_fourier, w_out, norm2_g, w_up, conv_w, conv_b, w_down):
    b, s, d = x.shape
    n_ctx = ctx.shape[1]
    depth = w_mod.shape[0]

    cc = jnp.zeros((8, d), F32).at[0:b].set(c).at[b].set(c_ctx)
    mod = _modulation(cc, w_mod, b_mod)

    cos_x, sin_x = _rope_tables(s)
    cos_c = jnp.ones((n_ctx, DIFF_WIDTH), F32)
    sin_c = jnp.zeros((n_ctx, DIFF_WIDTH), F32)

    cx = ctx
    for l in range(depth):
        lam_init = 0.8 - 0.6 * math.exp(-0.3 * l)
        mx = mod[l, 0:b].reshape(b, 1, 6, d)
        mc = jnp.broadcast_to(mod[l, b].reshape(1, 1, 6, d), (b, 1, 6, d))
        sh1, sc1, g1, sh2, sc2, g2 = (mx[:, :, t] for t in range(6))
        csh1, csc1, cg1, csh2, csc2, cg2 = (mc[:, :, t] for t in range(6))

        w_in_b = w_in[l].astype(BF16)
        w_out_b = w_out[l].astype(BF16)
        w_four_b = w_fourier[l].astype(BF16)
        n_up = w_up.shape[2] // FFN_TN
        w_up_b = w_up[l].astype(BF16)
        w_down_b = w_down[l].astype(BF16).reshape(n_up // 2, FFN_TN, d)
        cw = conv_w[l].reshape(3, n_up, FFN_TN).transpose(1, 0, 2)
        n1g = norm1_g[l].reshape(1, d)
        n2g = norm2_g[l].reshape(1, d)
        gq = jnp.tile(na_q_g[l], NA_HEADS).reshape(1, NA_WIDTH)
        gk = jnp.tile(na_k_g[l], NA_HEADS).reshape(1, NA_WIDTH)
        dgq = jnp.tile(diff_q_g[l], 2 * DIFF_HEADS).reshape(1, DIFF_WIDTH)
        dgk = jnp.tile(diff_k_g[l], 2 * DIFF_HEADS).reshape(1, DIFF_WIDTH)
        subg = jnp.tile(diff_subln_g[l], DIFF_HEADS).reshape(1, DIFF_WIDTH)
        cb = conv_b[l].reshape(n_up, 1, FFN_TN)

        naq, nak, nav, dqt, dk, dvt, fu, kabs = _in_projection(
            x, sh1, sc1, n1g, w_in_b, gq, gk, dgq, dgk, cos_x, sin_x, tm=1024)
        cnaq, cnak, cnav, cdqt, cdk, cdvt, cfu, ckabs = _in_projection(
            cx, csh1, csc1, n1g, w_in_b, gq, gk, dgq, dgk, cos_c, sin_c, tm=n_ctx)
        kmax_c = jnp.max(ckabs, axis=1).reshape(b, DIFF_WIDTH, 1)
        kmax_x = jnp.maximum(jnp.max(kabs, axis=1).reshape(b, DIFF_WIDTH, 1), kmax_c)

        na_x = _neighbourhood_attention(naq, nak, nav, cnak, cnav, na_rpb[l])
        diff_x = _diff_attention(diff_lambda[l], dqt, dk, dvt, kmax_x, lam_init, tq=128,
                                 ctx_kv=(cdk, cdvt.reshape(b, DIFF_WIDTH, n_ctx)))
        pr, pi = _position_dft(fu)
        x1, h2 = _merge(na_x, diff_x, pr, pi, x, g1, sh2, sc2, n2g, subg, w_four_b, w_out_b, lam_init, tm=512)
        x = _conv_ffn(h2, x1, g2, w_up_b, cw, cb, w_down_b, tm=512)

        if l < depth - 1:
            na_c = _dense_na_attention(cnaq, cnak, cnav)
            diff_c = _diff_attention(diff_lambda[l], cdqt, cdk, cdvt, kmax_c, lam_init, tq=128)
            cpr, cpi = _position_dft_direct(cfu)
            cx1, ch2 = _merge(na_c, diff_c, cpr, cpi, cx, cg1, csh2, csc2, n2g, subg, w_four_b, w_out_b,
                              lam_init, tm=n_ctx)
            cx = _conv_ffn(ch2, cx1, cg2, w_up_b, cw, cb, w_down_b, tm=n_ctx)
    return x
```

```python
import functools
import math

import numpy as np
import jax
import jax.numpy as jnp
from jax import lax
from jax.experimental import pallas as pl
from jax.experimental.pallas import tpu as pltpu

F32 = jnp.float32
BF16 = jnp.bfloat16

GRID_W = 64
EPS = 1e-6
ROPE_BASE = 10000.0
HEAD_DIM = 64
NA_HEADS = 8
NA_WIDTH = NA_HEADS * HEAD_DIM
NA_KH = 8
NA_KW = 16
DIFF_HEADS = 4
DIFF_QK_DIM = 32
DIFF_V_DIM = 64
DIFF_WIDTH = DIFF_HEADS * DIFF_V_DIM
FNET_GROUPS = 4
FNET_GROUP_DIM = 64
FNET_WIDTH = FNET_GROUPS * FNET_GROUP_DIM
NA_Q0 = 0
NA_K0 = NA_Q0 + NA_WIDTH
NA_V0 = NA_K0 + NA_WIDTH
DQ0 = NA_V0 + NA_WIDTH
DK0 = DQ0 + DIFF_WIDTH
DV0 = DK0 + DIFF_WIDTH
FN0 = DV0 + DIFF_WIDTH
IN_WIDTH = FN0 + FNET_WIDTH
NEG_BIG = -1e30
LOG2E = 1.4426950408889634

VMEM_LIMIT_BYTES = 52 * 1024 * 1024
LANES = 128
BF16_SUBLANES = 16


def _params(*sem, flags=None):
    return pltpu.CompilerParams(dimension_semantics=sem, vmem_limit_bytes=VMEM_LIMIT_BYTES, flags=flags)


def _dot(a, b):
    return jnp.dot(a, b, preferred_element_type=F32)


def _dot_nt(a, b):
    return lax.dot_general(a, b, (((1,), (1,)), ((), ())), preferred_element_type=F32)


def _sigmoid(x):
    return 1.0 / (1.0 + jnp.exp(-x))


def _group_mean_matrix(width, group):
    idx = np.arange(width) // group
    return jnp.asarray((idx[:, None] == idx[None, :]).astype(np.float32) / group, dtype=BF16)


def _mod_kernel(c_ref, w_ref, b_ref, o_ref):
    c = c_ref[...]
    s = c * _sigmoid(c)
    s_hi = s.astype(BF16)
    s_lo = (s - s_hi.astype(F32)).astype(BF16)
    w = w_ref[0]
    w_hi = w.astype(BF16)
    w_lo = (w - w_hi.astype(F32)).astype(BF16)
    o_ref[0] = _dot(s_hi, w_hi) + _dot(s_hi, w_lo) + _dot(s_lo, w_hi) + b_ref[0]


def _modulation(cc, w_mod, b_mod):
    depth, d, n = w_mod.shape
    tn = 768
    return pl.pallas_call(
        _mod_kernel,
        grid=(depth, n // tn),
        in_specs=[
            pl.BlockSpec((8, d), lambda l, j: (0, 0)),
            pl.BlockSpec((1, d, tn), lambda l, j: (l, 0, j)),
            pl.BlockSpec((1, 1, tn), lambda l, j: (l, 0, j)),
        ],
        out_specs=pl.BlockSpec((1, 8, tn), lambda l, j: (l, 0, j)),
        out_shape=jax.ShapeDtypeStruct((depth, 8, n), F32),
        compiler_params=_params("parallel", "parallel"),
        name="modulation",
    )(cc, w_mod, b_mod.reshape(depth, 1, n))


def _inproj_kernel(x_ref, sh_ref, sc_ref, g_ref, w_ref, gq_ref, gk_ref, dgq_ref, dgk_ref,
                   g64_ref, g32_ref, cos_ref, sin_ref,
                   naq_ref, nak_ref, nav_ref, dqt_ref, dk_ref, dvt_ref, fu_ref, kabs_ref):
    x = x_ref[0]
    ms = jnp.mean(x * x, axis=-1, keepdims=True)
    h = x * lax.rsqrt(ms + EPS) * g_ref[...]
    h = h * (1.0 + sc_ref[0]) + sh_ref[0]
    hb = h.astype(BF16)

    def proj(c0, c1):
        return _dot(hb, w_ref[:, c0:c1])

    def group_norm(p, gmat_ref, gain):
        sq = (p * p).astype(BF16)
        w = gmat_ref.shape[0]
        gms = jnp.concatenate([_dot(sq[:, c:c + w], gmat_ref[...]) for c in range(0, p.shape[1], w)], axis=1)
        return p * lax.rsqrt(gms + EPS) * gain

    def rope(y):
        lane = lax.broadcasted_iota(jnp.int32, y.shape, 1)
        nxt = pltpu.roll(y, DIFF_WIDTH - 8, axis=1)
        prv = pltpu.roll(y, 8, axis=1)
        partner = jnp.where((lane & 8) == 0, nxt, prv)
        return y * cos_ref[...] + partner * sin_ref[...]

    q = group_norm(proj(NA_Q0, NA_K0), g64_ref, gq_ref[...]) * (HEAD_DIM ** -0.5)
    naq_ref[0] = q.astype(BF16)
    k = group_norm(proj(NA_K0, NA_V0), g64_ref, gk_ref[...])
    nak_ref[0] = k.astype(BF16)
    nav_ref[0] = proj(NA_V0, DQ0).astype(BF16)
    dq = rope(group_norm(proj(DQ0, DK0), g32_ref, dgq_ref[...])) * (DIFF_QK_DIM ** -0.5 * LOG2E)
    dqt_ref[0] = dq.T.astype(BF16)
    dk = rope(group_norm(proj(DK0, DV0), g32_ref, dgk_ref[...]))
    dkb = dk.astype(BF16)
    dk_ref[0] = dkb
    kabs_ref[0, 0] = jnp.max(jnp.abs(dkb.astype(F32)), axis=0, keepdims=True)
    dvt_ref[0, 0] = proj(DV0, FN0).T.astype(BF16)
    fu_ref[0] = proj(FN0, IN_WIDTH).astype(BF16)


def _in_projection(x, shift, scale, norm_g, w_in, gq, gk, dgq, dgk, cos_t, sin_t, tm):
    b, s, d = x.shape
    g64 = _group_mean_matrix(NA_HALF, HEAD_DIM)
    g32 = _group_mean_matrix(DIFF_WIDTH, DIFF_QK_DIM)
    const = lambda shape: pl.BlockSpec(shape, lambda bi, i: (0,) * len(shape))
    per_b = pl.BlockSpec((1, 1, d), lambda bi, i: (bi, 0, 0))
    tok = lambda w: pl.BlockSpec((1, tm, w), lambda bi, i: (bi, i, 0))
    tok_t = lambda w: pl.BlockSpec((1, w, tm), lambda bi, i: (bi, 0, i))
    sds = jax.ShapeDtypeStruct
    return pl.pallas_call(
        _inproj_kernel,
        grid=(b, s // tm),
        in_specs=[
            tok(d), per_b, per_b, const((1, d)), const((d, IN_WIDTH)),
            const((1, NA_WIDTH)), const((1, NA_WIDTH)), const((1, DIFF_WIDTH)), const((1, DIFF_WIDTH)),
            const((NA_HALF, NA_HALF)), const((DIFF_WIDTH, DIFF_WIDTH)),
            pl.BlockSpec((tm, DIFF_WIDTH), lambda bi, i: (i, 0)),
            pl.BlockSpec((tm, DIFF_WIDTH), lambda bi, i: (i, 0)),
        ],
        out_specs=[tok(NA_WIDTH), tok(NA_WIDTH), tok(NA_WIDTH),
                   tok_t(DIFF_WIDTH), tok(DIFF_WIDTH),
                   pl.BlockSpec((1, 1, DIFF_WIDTH, tm), lambda bi, i: (bi, i, 0, 0)), tok(FNET_WIDTH),
                   pl.BlockSpec((1, 1, 1, DIFF_WIDTH), lambda bi, i: (bi, i, 0, 0))],
        out_shape=[sds((b, s, NA_WIDTH), BF16), sds((b, s, NA_WIDTH), BF16), sds((b, s, NA_WIDTH), BF16),
                   sds((b, DIFF_WIDTH, s), BF16), sds((b, s, DIFF_WIDTH), BF16),
                   sds((b, s // tm, DIFF_WIDTH, tm), BF16), sds((b, s, FNET_WIDTH), BF16),
                   sds((b, s // tm, 1, DIFF_WIDTH), F32)],
        compiler_params=_params("parallel", "parallel"),
        name="in_projection",
    )(x, shift, scale, norm_g, w_in, gq, gk, dgq, dgk, g64, g32, cos_t, sin_t)


def _rope_tables(s):
    pos = jnp.arange(s)
    rows = (pos // GRID_W).astype(F32)
    cols = (pos % GRID_W).astype(F32)
    m = DIFF_QK_DIM // 2
    inv = ROPE_BASE ** (-jnp.arange(0, m, 2, dtype=F32) / m)
    ang_r = rows[:, None] * inv[None, :]
    ang_c = cols[:, None] * inv[None, :]
    ang = jnp.concatenate([ang_r, ang_r, ang_c, ang_c], axis=1)
    sign = jnp.asarray(np.tile(np.repeat([-1.0, 1.0], 8), 2), F32)
    cos32 = jnp.cos(ang)
    sin32 = jnp.sin(ang) * sign[None, :]
    reps = DIFF_WIDTH // DIFF_QK_DIM
    return jnp.tile(cos32, (1, reps)), jnp.tile(sin32, (1, reps))


NA_GROUP_ROWS = 8
NA_HALF = NA_WIDTH // 2
NA_HEADS_PER_HALF = NA_HALF // HEAD_DIM


def _head_block_mask(rows_per_head, n_heads, width):
    r = lax.broadcasted_iota(jnp.int32, (rows_per_head * n_heads, width), 0) // rows_per_head
    c = lax.broadcasted_iota(jnp.int32, (rows_per_head * n_heads, width), 1) // (width // n_heads)
    return r == c


def _na_kernel(q_ref, kp_ref, k0_ref, kn_ref, vp_ref, v0_ref, vn_ref, kc_ref, vc_ref, bias_ref,
               o_ref, kbuf, vbuf, s_a, s_b):
    g = pl.program_id(1)
    blk = NA_GROUP_ROWS * GRID_W
    kbuf[0:blk] = kp_ref[0]
    kbuf[blk:2 * blk] = k0_ref[0]
    kbuf[2 * blk:3 * blk] = kn_ref[0]
    vbuf[0:blk] = vp_ref[0]
    vbuf[blk:2 * blk] = v0_ref[0]
    vbuf[2 * blk:3 * blk] = vn_ref[0]
    n_rows = pl.num_programs(1) * NA_GROUP_ROWS
    head_mask = _head_block_mask(GRID_W, NA_HEADS_PER_HALF, NA_HALF)
    n_keys = NA_KH * GRID_W

    s_bufs = (s_a, s_b)

    def window(i):
        r = g * NA_GROUP_ROWS + i
        row_start = jnp.clip(r - NA_KH // 2, 0, n_rows - NA_KH)
        off = pl.multiple_of((row_start - g * NA_GROUP_ROWS + NA_GROUP_ROWS) * GRID_W, GRID_W)
        return off, row_start - r + (NA_KH - 1)

    def scores(t, s_buf):
        i, hh = divmod(t, 2)
        off, d0 = window(i)
        cs = slice(hh * NA_HALF, (hh + 1) * NA_HALF)
        qh = q_ref[0, i * GRID_W:(i + 1) * GRID_W, cs]
        qst = jnp.concatenate([qh] * NA_HEADS_PER_HALF, axis=0)
        qst = jnp.where(head_mask, qst, jnp.zeros_like(qst))
        bias = jnp.concatenate([bias_ref[d0 + 2 * j, hh] for j in range(NA_KH // 2)], axis=1)
        s_buf[:, 0:n_keys] = _dot_nt(qst, kbuf[pl.ds(off, n_keys), cs]) + bias
        s_buf[:, n_keys:] = _dot_nt(qst, kc_ref[0, :, cs])

    def finish(t, s_buf):
        i, hh = divmod(t, 2)
        off, _ = window(i)
        cs = slice(hh * NA_HALF, (hh + 1) * NA_HALF)
        s = s_buf[...]
        m = jnp.max(s, axis=-1, keepdims=True)
        e = jnp.exp(s - m)
        l = jnp.sum(e, axis=-1, keepdims=True)
        eb = e.astype(BF16)
        o = _dot(eb[:, 0:n_keys], vbuf[pl.ds(off, n_keys), cs]) + _dot(eb[:, n_keys:], vc_ref[0, :, cs])
        o = jnp.where(head_mask, o * (1.0 / l), 0.0)
        out = o[0:GRID_W]
        for hl in range(1, NA_HEADS_PER_HALF):
            out = out + o[hl * GRID_W:(hl + 1) * GRID_W]
        o_ref[0, i * GRID_W:(i + 1) * GRID_W, cs] = out.astype(o_ref.dtype)

    n_jobs = 2 * NA_GROUP_ROWS
    scores(0, s_a)
    for t in range(n_jobs):
        if t + 1 < n_jobs:
            scores(t + 1, s_bufs[(t + 1) % 2])
        finish(t, s_bufs[t % 2])


def _na_bias_table(rpb):
    h, n_dr, _ = rpb.shape
    qc = np.arange(GRID_W)
    kc = np.arange(GRID_W)
    col_start = np.clip(qc - NA_KW // 2, 0, GRID_W - NA_KW)
    valid = (kc[None, :] >= col_start[:, None]) & (kc[None, :] < col_start[:, None] + NA_KW)
    pad = GRID_W - NA_KW
    period = 2 * GRID_W - 1
    rext = jnp.pad(rpb, ((0, 0), (0, 0), (pad, pad)))
    flat = jnp.tile(rext, (1, 1, GRID_W + 1))[:, :, :GRID_W * (period + 1)]
    hankel = flat.reshape(h, n_dr, GRID_W, period + 1)[..., :GRID_W]
    toep = hankel[:, :, ::-1, :]
    toep = jnp.where(jnp.asarray(valid)[None, None], toep, NEG_BIG)
    pair = jnp.concatenate([toep[:, :-1], toep[:, 1:]], axis=-1)
    pair = pair.reshape(2, NA_HEADS_PER_HALF, n_dr - 1, GRID_W, 2 * GRID_W).transpose(2, 0, 1, 3, 4)
    return pair.reshape(n_dr - 1, 2, NA_HEADS_PER_HALF * GRID_W, 2 * GRID_W).astype(F32)


def _neighbourhood_attention(q, k, v, kc, vc, rpb):
    b, s, _ = q.shape
    n_groups = s // (NA_GROUP_ROWS * GRID_W)
    blk = NA_GROUP_ROWS * GRID_W
    c = kc.shape[1]
    bias = _na_bias_table(rpb)
    cur = pl.BlockSpec((1, blk, NA_WIDTH), lambda bi, g: (bi, g, 0))
    prv = pl.BlockSpec((1, blk, NA_WIDTH), lambda bi, g: (bi, jnp.maximum(g - 1, 0), 0))
    nxt = pl.BlockSpec((1, blk, NA_WIDTH), lambda bi, g: (bi, jnp.minimum(g + 1, n_groups - 1), 0))
    ctx = pl.BlockSpec((1, c, NA_WIDTH), lambda bi, g: (bi, 0, 0))
    return pl.pallas_call(
        _na_kernel,
        grid=(b, n_groups),
        in_specs=[cur, prv, cur, nxt, prv, cur, nxt, ctx, ctx,
                  pl.BlockSpec(bias.shape, lambda bi, g: (0, 0, 0, 0))],
        out_specs=cur,
        out_shape=jax.ShapeDtypeStruct((b, s, NA_WIDTH), BF16),
        scratch_shapes=[pltpu.VMEM((3 * blk, NA_WIDTH), BF16), pltpu.VMEM((3 * blk, NA_WIDTH), BF16)]
                       + [pltpu.VMEM((NA_HEADS_PER_HALF * GRID_W, NA_KH * GRID_W + c), F32)] * 2,
        compiler_params=_params("parallel", "arbitrary"),
        name="neighbourhood_attention",
    )(q, k, k, k, v, v, v, kc, vc, bias)


def _dense_na_kernel(q_ref, k_ref, v_ref, o_ref):
    c = q_ref.shape[1]
    head_mask = _head_block_mask(c, NA_HEADS_PER_HALF, NA_HALF)
    for hh in range(2):
        cs = slice(hh * NA_HALF, (hh + 1) * NA_HALF)
        qh = q_ref[0, :, cs]
        qst = jnp.concatenate([qh] * NA_HEADS_PER_HALF, axis=0)
        qst = jnp.where(head_mask, qst, jnp.zeros_like(qst))
        s = _dot_nt(qst, k_ref[0, :, cs])
        m = jnp.max(s, axis=-1, keepdims=True)
        e = jnp.exp(s - m)
        l = jnp.sum(e, axis=-1, keepdims=True)
        o = _dot(e.astype(BF16), v_ref[0, :, cs])
        o = jnp.where(head_mask, o * (1.0 / l), 0.0)
        out = o[0:c]
        for hl in range(1, NA_HEADS_PER_HALF):
            out = out + o[hl * c:(hl + 1) * c]
        o_ref[0, :, cs] = out.astype(o_ref.dtype)


def _dense_na_attention(q, k, v):
    b, c, w = q.shape
    spec = pl.BlockSpec((1, c, w), lambda bi: (bi, 0, 0))
    return pl.pallas_call(
        _dense_na_kernel,
        grid=(b,),
        in_specs=[spec, spec, spec],
        out_specs=spec,
        out_shape=jax.ShapeDtypeStruct((b, c, w), BF16),
        compiler_params=_params("parallel"),
        name="context_dense_attention",
    )(q, k, v)


N_DIFF_STREAMS = 2 * DIFF_HEADS


DIFF_HEAD_PAIRS = DIFF_HEADS // 2
DIFF_MIN_TRUSTED_SUM = 2.0 ** -80


def _diff_kernel(*refs, lam_init, has_ctx):
    n_in = 7 if has_ctx else 5
    lam_ref, qt_ref, k_ref, vt_ref, kmax_ref = refs[:5]
    kc_ref, vct_ref = refs[5:7] if has_ctx else (None, None)
    o_ref, qs, s_a, s_b, p_a, p_b, m_scr, l_scr, acc = refs[n_in:]
    tq = qt_ref.shape[2]
    n_chunks, _, tk = vt_ref.shape[1:]
    pair_w = 4 * tq

    qt = qt_ref[0]
    stream = lax.broadcasted_iota(jnp.int32, qt.shape, 0) // DIFF_QK_DIM
    for st in range(N_DIFF_STREAMS):
        qs[:, st * tq:(st + 1) * tq] = jnp.where(stream == st, qt, jnp.zeros_like(qt))
    bound = jnp.sum(jnp.abs(qs[...].astype(F32)) * kmax_ref[0], axis=0, keepdims=True)
    m_scr[...] = bound * (1.0 + 2.0 ** -10) + 2.0 ** -10
    l_scr[...] = jnp.zeros(l_scr.shape, F32)
    acc[...] = jnp.zeros(acc.shape, F32)

    def scores(kk, s_buf):
        n = kk.shape[0]
        for pair in range(DIFF_HEAD_PAIRS):
            cols = slice(pair * pair_w, (pair + 1) * pair_w)
            s_buf[0:n, cols] = _dot(kk, qs[:, cols])

    def probs_bounded(kk, p_buf):
        n = kk.shape[0]
        for pair in range(DIFF_HEAD_PAIRS):
            cols = slice(pair * pair_w, (pair + 1) * pair_w)
            p = jnp.exp2(_dot(kk, qs[:, cols]) - m_scr[:, cols])
            l_scr[:, cols] += jnp.sum(p, axis=0, keepdims=True)
            p_buf[0:n, cols] = p.astype(BF16)

    def values_bounded(p_buf, n, vt_rows):
        for h in range(DIFF_HEADS):
            rows = slice(h * DIFF_V_DIM, (h + 1) * DIFF_V_DIM)
            cols = slice(2 * h * tq, (2 * h + 2) * tq)
            acc[rows, :] += _dot(vt_rows(rows), p_buf[0:n, cols])

    def absorb(s_buf, n, vt_rows):
        for pair in range(DIFF_HEAD_PAIRS):
            cols = slice(pair * pair_w, (pair + 1) * pair_w)
            s = s_buf[0:n, cols]
            m_old = m_scr[:, cols]
            m_new = jnp.maximum(m_old, jnp.max(s, axis=0, keepdims=True))
            alpha = jnp.exp2(m_old - m_new)
            p = jnp.exp2(s - m_new)
            l_scr[:, cols] = alpha * l_scr[:, cols] + jnp.sum(p, axis=0, keepdims=True)
            m_scr[:, cols] = m_new
            pb = p.astype(BF16)
            for hh in range(2):
                h = 2 * pair + hh
                rows = slice(h * DIFF_V_DIM, (h + 1) * DIFF_V_DIM)
                hc = slice(hh * 2 * tq, (hh + 1) * 2 * tq)
                acc[rows, :] = acc[rows, :] * alpha[:, hc] + _dot(vt_rows(rows), pb[:, hc])

    n_items = n_chunks + (1 if has_ctx else 0)

    def all_items(scores_fn, absorb_fn, bufs):
        def scores_item(t, buf):
            if has_ctx and isinstance(t, int) and t == n_chunks:
                scores_fn(kc_ref[0], buf)
            else:
                start = t * tk if isinstance(t, int) else pl.multiple_of(t * tk, tk)
                scores_fn(k_ref[0, pl.ds(start, tk), :], buf)

        def absorb_item(t, buf):
            if has_ctx and isinstance(t, int) and t == n_chunks:
                absorb_fn(buf, kc_ref.shape[1], lambda rows: vct_ref[0, rows, :])
            else:
                absorb_fn(buf, tk, lambda rows: vt_ref[0, t, rows, :])

        scores_item(0, bufs[0])
        n_pairs = (n_chunks - 1) // 2

        def body(i, carry):
            t = 2 * i
            scores_item(t + 1, bufs[1])
            absorb_item(t, bufs[0])
            scores_item(t + 2, bufs[0])
            absorb_item(t + 1, bufs[1])
            return carry

        lax.fori_loop(0, n_pairs, body, 0)
        for t in range(2 * n_pairs, n_items):
            if t + 1 < n_items:
                scores_item(t + 1, bufs[(t + 1) % 2])
            absorb_item(t, bufs[t % 2])

    all_items(probs_bounded, values_bounded, (p_a, p_b))

    @pl.when(jnp.min(l_scr[...]) < DIFF_MIN_TRUSTED_SUM)
    def _():
        m_scr[...] = jnp.full(m_scr.shape, NEG_BIG, F32)
        l_scr[...] = jnp.zeros(l_scr.shape, F32)
        acc[...] = jnp.zeros(acc.shape, F32)
        all_items(scores, absorb, (s_a, s_b))

    lp = lam_ref[...]
    lam = (jnp.exp(jnp.sum(lp[0:1] * lp[1:2], axis=1, keepdims=True))
           - jnp.exp(jnp.sum(lp[2:3] * lp[3:4], axis=1, keepdims=True)) + lam_init)
    inv_l = 1.0 / l_scr[...]
    outs = []
    for h in range(DIFF_HEADS):
        rows = slice(h * DIFF_V_DIM, (h + 1) * DIFF_V_DIM)
        o1 = acc[rows, 0:tq] * inv_l[:, 2 * h * tq:(2 * h + 1) * tq]
        o2 = acc[rows, tq:2 * tq] * inv_l[:, (2 * h + 1) * tq:(2 * h + 2) * tq]
        outs.append(o1 - lam * o2)
    o_ref[0] = jnp.concatenate(outs, axis=0).T


def _diff_attention(lam_p, qt, k, vt, kmax, lam_init, tq, ctx_kv=None):
    b, w, sq = qt.shape
    n_chunks, _, tk = vt.shape[1:]
    whole = lambda a: pl.BlockSpec((1,) + a.shape[1:], lambda bi, i: (bi,) + (0,) * (a.ndim - 1))
    operands = [lam_p, qt, k, vt, kmax]
    in_specs = [
        pl.BlockSpec(lam_p.shape, lambda bi, i: (0, 0)),
        pl.BlockSpec((1, w, tq), lambda bi, i: (bi, 0, i)),
        whole(k), whole(vt), whole(kmax),
    ]
    if ctx_kv is not None:
        operands += list(ctx_kv)
        in_specs += [whole(ctx_kv[0]), whole(ctx_kv[1])]
    return pl.pallas_call(
        functools.partial(_diff_kernel, lam_init=lam_init, has_ctx=ctx_kv is not None),
        grid=(b, sq // tq),
        in_specs=in_specs,
        out_specs=pl.BlockSpec((1, tq, w), lambda bi, i: (bi, i, 0)),
        out_shape=jax.ShapeDtypeStruct((b, sq, w), F32),
        scratch_shapes=[
            pltpu.VMEM((w, N_DIFF_STREAMS * tq), BF16),
            pltpu.VMEM((tk, N_DIFF_STREAMS * tq), F32),
            pltpu.VMEM((tk, N_DIFF_STREAMS * tq), F32),
            pltpu.VMEM((tk, N_DIFF_STREAMS * tq), BF16),
            pltpu.VMEM((tk, N_DIFF_STREAMS * tq), BF16),
            pltpu.VMEM((1, N_DIFF_STREAMS * tq), F32),
            pltpu.VMEM((1, N_DIFF_STREAMS * tq), F32),
            pltpu.VMEM((w, 2 * tq), F32),
        ],
        compiler_params=_params("parallel", "arbitrary"),
        name="differential_attention",
    )(*operands)


FFT_N1 = 64
FFT_N1_STEP = 8


def _dft_tables(n):
    idx = np.arange(n)
    ang = 2.0 * np.pi * ((idx[:, None] * idx[None, :]) % n) / n
    return np.cos(ang), np.sin(ang)


def _fft_stage_a_kernel(x_ref, t_ref, twc_ref, tws_ref, ur_ref, ui_ref):
    n2 = x_ref.shape[1]
    t = t_ref[...].astype(BF16)
    for i in range(FFT_N1_STEP):
        xs = x_ref[0, :, i * FNET_WIDTH:(i + 1) * FNET_WIDTH]
        u = _dot(t, xs)
        ur, ui = u[0:n2], u[n2:2 * n2]
        c = jnp.concatenate([twc_ref[i]] * (FNET_WIDTH // LANES), axis=1)
        s = jnp.concatenate([tws_ref[i]] * (FNET_WIDTH // LANES), axis=1)
        ur_ref[0, i] = (ur * c + ui * s).astype(BF16)
        ui_ref[0, i] = (ui * c - ur * s).astype(BF16)


def _fft_stage_c_kernel(ur_ref, ui_ref, c_ref, s_ref, pr_ref, pi_ref):
    ur = ur_ref[0]
    ui = ui_ref[0]
    c1 = c_ref[...].astype(BF16)
    s1 = s_ref[...].astype(BF16)
    pr_ref[0] = (_dot(c1, ur) + _dot(s1, ui)).astype(BF16)
    pi_ref[0] = (_dot(c1, ui) - _dot(s1, ur)).astype(BF16)


def _position_dft(fu):
    b, l, w = fu.shape
    n1, n2 = FFT_N1, l // FFT_N1
    scale = 1.0 / math.sqrt(l * FNET_GROUP_DIM)
    c2, s2 = _dft_tables(n2)
    t2 = jnp.asarray(np.concatenate([c2, -s2], axis=0) * scale, dtype=F32)
    ang = (2.0 * jnp.pi / l) * (jnp.arange(n1, dtype=F32)[:, None] * jnp.arange(n2, dtype=F32)[None, :])
    twc = jnp.broadcast_to(jnp.cos(ang)[:, :, None], (n1, n2, LANES))
    tws = jnp.broadcast_to(jnp.sin(ang)[:, :, None], (n1, n2, LANES))
    xv = fu.reshape(b, n2, n1 * w)
    sds = jax.ShapeDtypeStruct
    ur, ui = pl.pallas_call(
        _fft_stage_a_kernel,
        grid=(b, n1 // FFT_N1_STEP),
        in_specs=[
            pl.BlockSpec((1, n2, FFT_N1_STEP * w), lambda bi, j: (bi, 0, j)),
            pl.BlockSpec((2 * n2, n2), lambda bi, j: (0, 0)),
            pl.BlockSpec((FFT_N1_STEP, n2, LANES), lambda bi, j: (j, 0, 0)),
            pl.BlockSpec((FFT_N1_STEP, n2, LANES), lambda bi, j: (j, 0, 0)),
        ],
        out_specs=[pl.BlockSpec((1, FFT_N1_STEP, n2, w), lambda bi, j: (bi, j, 0, 0))] * 2,
        out_shape=[sds((b, n1, n2, w), BF16)] * 2,
        compiler_params=_params("parallel", "parallel"),
        name="position_dft_stage_a",
    )(xv, t2, twc, tws)
    c1, s1 = _dft_tables(n1)
    tn = 4096
    cols = n2 * w
    blk = pl.BlockSpec((1, n1, tn), lambda bi, j: (bi, 0, j))
    tab = pl.BlockSpec((n1, n1), lambda bi, j: (0, 0))
    pr, pi = pl.pallas_call(
        _fft_stage_c_kernel,
        grid=(b, cols // tn),
        in_specs=[blk, blk, tab, tab],
        out_specs=[blk, blk],
        out_shape=[sds((b, n1, cols), BF16)] * 2,
        compiler_params=_params("parallel", "parallel"),
        name="position_dft_stage_c",
    )(ur.reshape(b, n1, cols), ui.reshape(b, n1, cols), jnp.asarray(c1, F32), jnp.asarray(s1, F32))
    return pr.reshape(b, l, w), pi.reshape(b, l, w)


def _direct_dft_kernel(x_ref, t_ref, pr_ref, pi_ref):
    l = x_ref.shape[1]
    u = _dot(t_ref[...].astype(BF16), x_ref[0])
    pr_ref[0] = u[0:l].astype(BF16)
    pi_ref[0] = u[l:2 * l].astype(BF16)


def _position_dft_direct(fu):
    b, l, w = fu.shape
    scale = 1.0 / math.sqrt(l * FNET_GROUP_DIM)
    c, s = _dft_tables(l)
    t = jnp.asarray(np.concatenate([c, -s], axis=0) * scale, dtype=F32)
    spec = pl.BlockSpec((1, l, w), lambda bi: (bi, 0, 0))
    return pl.pallas_call(
        _direct_dft_kernel,
        grid=(b,),
        in_specs=[spec, pl.BlockSpec((2 * l, l), lambda bi: (0, 0))],
        out_specs=[spec, spec],
        out_shape=[jax.ShapeDtypeStruct((b, l, w), BF16)] * 2,
        compiler_params=_params("parallel"),
        name="context_position_dft",
    )(fu, t)


def _merge_kernel(na_ref, df_ref, pr_ref, pi_ref, x_ref, g1_ref, sh2_ref, sc2_ref, n2g_ref, subg_ref,
                  g64_ref, wc_ref, ws_ref, wf_ref, wo_ref, x1_ref, h2_ref, *, diff_scale):
    d = df_ref[0]
    gms = _dot((d * d).astype(BF16), g64_ref[...])
    dn = d * lax.rsqrt(gms + EPS) * subg_ref[...] * diff_scale
    fr = _dot(pr_ref[0], wc_ref[...].astype(BF16)) + _dot(pi_ref[0], ws_ref[...].astype(BF16))
    four = _dot(fr.astype(BF16), wf_ref[...])
    y = (_dot(na_ref[0], wo_ref[0:NA_WIDTH, :])
         + _dot(dn.astype(BF16), wo_ref[NA_WIDTH:NA_WIDTH + DIFF_WIDTH, :])
         + _dot(four.astype(BF16), wo_ref[NA_WIDTH + DIFF_WIDTH:, :]))
    x1 = x_ref[0] + g1_ref[0] * y
    x1_ref[0] = x1
    ms = jnp.mean(x1 * x1, axis=-1, keepdims=True)
    h = x1 * lax.rsqrt(ms + EPS) * n2g_ref[...]
    h2_ref[0] = (h * (1.0 + sc2_ref[0]) + sh2_ref[0]).astype(BF16)


def _channel_dft_tables():
    c, s = _dft_tables(FNET_GROUP_DIM)
    eye = np.eye(FNET_GROUPS)
    return jnp.asarray(np.kron(eye, c), F32), jnp.asarray(np.kron(eye, s), F32)


def _merge(na_o, diff_o, pr, pi, x, g1, sh2, sc2, norm2_g, subln_g, w_four, w_out, lam_init, tm):
    b, s, d = x.shape
    wc, ws = _channel_dft_tables()
    g64 = _group_mean_matrix(DIFF_WIDTH, DIFF_V_DIM)
    const = lambda shape: pl.BlockSpec(shape, lambda bi, i: (0,) * len(shape))
    per_b = pl.BlockSpec((1, 1, d), lambda bi, i: (bi, 0, 0))
    tok = lambda w: pl.BlockSpec((1, tm, w), lambda bi, i: (bi, i, 0))
    sq = (FNET_WIDTH, FNET_WIDTH)
    return pl.pallas_call(
        functools.partial(_merge_kernel, diff_scale=1.0 - lam_init),
        grid=(b, s // tm),
        in_specs=[tok(NA_WIDTH), tok(DIFF_WIDTH), tok(FNET_WIDTH), tok(FNET_WIDTH), tok(d),
                  per_b, per_b, per_b, const((1, d)), const((1, DIFF_WIDTH)),
                  const((DIFF_WIDTH, DIFF_WIDTH)), const(sq), const(sq), const(sq), const((d, d))],
        out_specs=[tok(d), tok(d)],
        out_shape=[jax.ShapeDtypeStruct((b, s, d), F32), jax.ShapeDtypeStruct((b, s, d), BF16)],
        compiler_params=_params("parallel", "parallel"),
        name="merge_projection",
    )(na_o, diff_o, pr, pi, x, g1, sh2, sc2, norm2_g, subln_g, g64, wc, ws, w_four, w_out)


FFN_TN = 256
FFN_HALO = BF16_SUBLANES


def _ffn_kernel(hp_ref, h_ref, hn_ref, wu_ref, cw_ref, cb_ref, wd_ref, x1_ref, g2_ref, o_ref,
                hcat, ug_a, uv_a, ug_b, uv_b, a_a, a_b, acc):
    i = pl.program_id(1)
    tm = h_ref.shape[1]
    nj = wd_ref.shape[0]
    hp = hp_ref[0]
    hn = hn_ref[0]
    hcat[0:FFN_HALO] = jnp.where(i > 0, hp, jnp.zeros_like(hp))
    hcat[FFN_HALO:FFN_HALO + tm] = h_ref[0]
    hcat[FFN_HALO + tm:] = jnp.where(i < pl.num_programs(1) - 1, hn, jnp.zeros_like(hn))
    acc[...] = jnp.zeros(acc.shape, F32)
    bufs = ((ug_a, uv_a), (ug_b, uv_b))
    a_bufs = (a_a, a_b)

    def project_up(j, buf):
        gc = slice(j * FFN_TN, (j + 1) * FFN_TN)
        vc = slice((nj + j) * FFN_TN, (nj + j + 1) * FFN_TN)
        buf[0][...] = _dot(hcat[...], wu_ref[:, gc])
        buf[1][...] = _dot(hcat[...], wu_ref[:, vc])

    def conv(u, cw, cb):
        total = u.shape[0]
        prv = pltpu.roll(u, 1, axis=0)[FFN_HALO:FFN_HALO + tm]
        cur = u[FFN_HALO:FFN_HALO + tm]
        nxt = pltpu.roll(u, total - 1, axis=0)[FFN_HALO:FFN_HALO + tm]
        return prv * cw[0:1] + cur * cw[1:2] + nxt * cw[2:3] + cb

    def gate(j, buf, a_buf):
        g = conv(buf[0][...], cw_ref[j], cb_ref[j])
        val = conv(buf[1][...], cw_ref[nj + j], cb_ref[nj + j])
        a_buf[...] = (g * _sigmoid(g) * val).astype(BF16)

    def step(t, par, do_up, do_gate, do_down):
        if do_up:
            project_up(t + 1, bufs[1 - par])
        if do_gate:
            gate(t, bufs[par], a_bufs[par])
        if do_down:
            acc[...] += _dot(a_bufs[1 - par][...], wd_ref[t - 1])

    project_up(0, bufs[0])
    for t in range(nj + 1):
        step(t, t % 2, t + 1 < nj, t < nj, t > 0)
    o_ref[0] = x1_ref[0] + g2_ref[0] * acc[...]


def _conv_ffn(h2, x1, g2, w_up, conv_w, conv_b, w_down, tm):
    b, s, d = x1.shape
    nj = w_down.shape[0]
    per_tile = tm // FFN_HALO
    n_halo_blocks = s // FFN_HALO
    rows = tm + 2 * FFN_HALO
    resident = lambda a: pl.BlockSpec(a.shape, lambda bi, i: (0,) * a.ndim, pipeline_mode=pl.Buffered(1))
    return pl.pallas_call(
        _ffn_kernel,
        grid=(b, s // tm),
        in_specs=[
            pl.BlockSpec((1, FFN_HALO, d), lambda bi, i: (bi, jnp.maximum(i * per_tile - 1, 0), 0)),
            pl.BlockSpec((1, tm, d), lambda bi, i: (bi, i, 0)),
            pl.BlockSpec((1, FFN_HALO, d), lambda bi, i: (bi, jnp.minimum((i + 1) * per_tile, n_halo_blocks - 1), 0)),
            resident(w_up), resident(conv_w), resident(conv_b), resident(w_down),
            pl.BlockSpec((1, tm, d), lambda bi, i: (bi, i, 0)),
            pl.BlockSpec((1, 1, d), lambda bi, i: (bi, 0, 0)),
        ],
        out_specs=pl.BlockSpec((1, tm, d), lambda bi, i: (bi, i, 0)),
        out_shape=jax.ShapeDtypeStruct((b, s, d), F32),
        scratch_shapes=[pltpu.VMEM((rows, d), BF16)] + [pltpu.VMEM((rows, FFN_TN), F32)] * 4
                       + [pltpu.VMEM((tm, FFN_TN), BF16)] * 2 + [pltpu.VMEM((tm, d), F32)],
        compiler_params=_params("parallel", "arbitrary"),
        name="conv_ffn",
    )(h2, h2, h2, w_up, conv_w, conv_b, w_down, x1, g2)


def kernel(x, c, ctx, c_ctx, w_mod, b_mod, norm1_g, w_in, na_q_g, na_k_g, na_rpb, diff_q_g, diff_k_g,
           diff_lambda, diff_subln_g, w_fourier, w_out, norm2_g, w_up, conv_w, conv_b, w_down):
    b, s, d = x.shape
    n_ctx = ctx.shape[1]
    depth = w_mod.shape[0]

    cc = jnp.zeros((8, d), F32).at[0:b].set(c).at[b].set(c_ctx)
    mod = _modulation(cc, w_mod, b_mod)

    cos_x, sin_x = _rope_tables(s)
    cos_c = jnp.ones((n_ctx, DIFF_WIDTH), F32)
    sin_c = jnp.zeros((n_ctx, DIFF_WIDTH), F32)

    cx = ctx
    for l in range(depth):
        lam_init = 0.8 - 0.6 * math.exp(-0.3 * l)
        mx = mod[l, 0:b].reshape(b, 1, 6, d)
        mc = jnp.broadcast_to(mod[l, b].reshape(1, 1, 6, d), (b, 1, 6, d))
        sh1, sc1, g1, sh2, sc2, g2 = (mx[:, :, t] for t in range(6))
        csh1, csc1, cg1, csh2, csc2, cg2 = (mc[:, :, t] for t in range(6))

        w_in_b = w_in[l].astype(BF16)
        w_out_b = w_out[l].astype(BF16)
        w_four_b = w_fourier[l].astype(BF16)
        n_up = w_up.shape[2] // FFN_TN
        w_up_b = w_up[l].astype(BF16)
        w_down_b = w_down[l].astype(BF16).reshape(n_up // 2, FFN_TN, d)
        cw = conv_w[l].reshape(3, n_up, FFN_TN).transpose(1, 0, 2)
        n1g = norm1_g[l].reshape(1, d)
        n2g = norm2_g[l].reshape(1, d)
        gq = jnp.tile(na_q_g[l], NA_HEADS).reshape(1, NA_WIDTH)
        gk = jnp.tile(na_k_g[l], NA_HEADS).reshape(1, NA_WIDTH)
        dgq = jnp.tile(diff_q_g[l], 2 * DIFF_HEADS).reshape(1, DIFF_WIDTH)
        dgk = jnp.tile(diff_k_g[l], 2 * DIFF_HEADS).reshape(1, DIFF_WIDTH)
        subg = jnp.tile(diff_subln_g[l], DIFF_HEADS).reshape(1, DIFF_WIDTH)
        cb = conv_b[l].reshape(n_up, 1, FFN_TN)

        naq, nak, nav, dqt, dk, dvt, fu, kabs = _in_projection(
            x, sh1, sc1, n1g, w_in_b, gq, gk, dgq, dgk, cos_x, sin_x, tm=1024)
        cnaq, cnak, cnav, cdqt, cdk, cdvt, cfu, ckabs = _in_projection(
            cx, csh1, csc1, n1g, w_in_b, gq, gk, dgq, dgk, cos_c, sin_c, tm=n_ctx)
        kmax_c = jnp.max(ckabs, axis=1).reshape(b, DIFF_WIDTH, 1)
        kmax_x = jnp.maximum(jnp.max(kabs, axis=1).reshape(b, DIFF_WIDTH, 1), kmax_c)

        na_x = _neighbourhood_attention(naq, nak, nav, cnak, cnav, na_rpb[l])
        diff_x = _diff_attention(diff_lambda[l], dqt, dk, dvt, kmax_x, lam_init, tq=128,
                                 ctx_kv=(cdk, cdvt.reshape(b, DIFF_WIDTH, n_ctx)))
        pr, pi = _position_dft(fu)
        x1, h2 = _merge(na_x, diff_x, pr, pi, x, g1, sh2, sc2, n2g, subg, w_four_b, w_out_b, lam_init, tm=512)
        x = _conv_ffn(h2, x1, g2, w_up_b, cw, cb, w_down_b, tm=512)

        if l < depth - 1:
            na_c = _dense_na_attention(cnaq, cnak, cnav)
            diff_c = _diff_attention(diff_lambda[l], cdqt, cdk, cdvt, kmax_c, lam_init, tq=128)
            cpr, cpi = _position_dft_direct(cfu)
            cx1, ch2 = _merge(na_c, diff_c, cpr, cpi, cx, cg1, csh2, csc2, n2g, subg, w_four_b, w_out_b,
                              lam_init, tm=n_ctx)
            cx = _conv_ffn(ch2, cx1, cg2, w_up_b, cw, cb, w_down_b, tm=n_ctx)
    return x
```
